```python
import math
import jax, jax.numpy as jnp
from jax import lax
import numpy as np

D_MODEL = 2048
BATCH = 1
SEQ = 8192
DEPTH = 1
DEC_BATCH = 2
DEC_SEQ = 4096
PAST_LEN = 128

EPS = 1e-6
NEG = -1e30
HEAD_DIM = 128
ATTN_GROUPS = ((128, 1), (512, 4), (2048, 16))
HEADS_PER_GROUP = 4
N_ATTN_HEADS = HEADS_PER_GROUP * len(ATTN_GROUPS)
ATTN_WIDTH = N_ATTN_HEADS * HEAD_DIM
ROPE_THETA = 500000.0
ROPE_DIM = HEAD_DIM // 4
SGU_CHUNK = 128
SGU_GROUPS = 4
SGU_WIDTH = 1536
SGU_GROUP_WIDTH = SGU_WIDTH // SGU_GROUPS
N_MEM = 256
MEM_HEADS = 4
MEM_HEAD_DIM = 256
MEM_WIDTH = MEM_HEADS * MEM_HEAD_DIM
N_BRANCHES = 3
IN_WIDTH = 3 * ATTN_WIDTH + 2 * SGU_WIDTH + MEM_WIDTH + N_BRANCHES * D_MODEL
SPLIT_POINTS = (ATTN_WIDTH, 2 * ATTN_WIDTH, 3 * ATTN_WIDTH,
                3 * ATTN_WIDTH + SGU_WIDTH, 3 * ATTN_WIDTH + 2 * SGU_WIDTH,
                3 * ATTN_WIDTH + 2 * SGU_WIDTH + MEM_WIDTH)
N_EXPERTS = 16
EXPERT_FF = 5632
CAPACITY_FACTOR = 2

kernel_name = "hybrid_dilated_sgu_memory_ecmoe_encoder"


def rmsnorm(x, g):
    xf = x.astype(jnp.float32)
    y = xf * lax.rsqrt(jnp.mean(xf * xf, axis=-1, keepdims=True) + EPS)
    return (y * g.astype(jnp.float32)).astype(x.dtype)


def partial_rope(x, pos):
    half = ROPE_DIM // 2
    inv = ROPE_THETA ** (-jnp.arange(0, ROPE_DIM, 2, dtype=jnp.float32) / ROPE_DIM)
    ang = pos[:, None] * inv[None, :]
    cos = jnp.cos(ang)[None, :, None, :]
    sin = jnp.sin(ang)[None, :, None, :]
    xr = x[..., :ROPE_DIM].astype(jnp.float32)
    x1, x2 = xr[..., :half], xr[..., half:]
    rot = jnp.concatenate([x1 * cos - x2 * sin, x2 * cos + x1 * sin], axis=-1)
    return jnp.concatenate([rot.astype(x.dtype), x[..., ROPE_DIM:]], axis=-1)


def band_attention(q, k, v, w):
    N, L, H, dh = q.shape
    nb = -(-L // w)
    pad = nb * w - L
    qp = jnp.pad(q, ((0, 0), (0, pad), (0, 0), (0, 0))).reshape(N, nb, w, H, dh)
    kp = jnp.pad(k, ((0, 0), (w, pad + w), (0, 0), (0, 0))).reshape(N, nb + 2, w, H, dh)
    vp = jnp.pad(v, ((0, 0), (w, pad + w), (0, 0), (0, 0))).reshape(N, nb + 2, w, H, dh)
    kb = jnp.concatenate([kp[:, :-2], kp[:, 1:-1], kp[:, 2:]], axis=2)
    vb = jnp.concatenate([vp[:, :-2], vp[:, 1:-1], vp[:, 2:]], axis=2)
    s = jnp.einsum("nbqhd,nbkhd->nbhqk", qp, kb).astype(jnp.float32) * (1.0 / math.sqrt(dh))
    qpos = jnp.arange(nb)[:, None] * w + jnp.arange(w)[None, :]
    kpos = (jnp.arange(nb)[:, None] - 1) * w + jnp.arange(3 * w)[None, :]
    rel = kpos[:, None, :] - qpos[:, :, None]
    valid = (jnp.abs(rel) <= w) & (kpos[:, None, :] >= 0) & (kpos[:, None, :] < L)
    s = jnp.where(valid[None, :, None], s, NEG)
    lse = jax.nn.logsumexp(s, axis=-1)
    p = jnp.exp(s - lse[..., None])
    o = jnp.einsum("nbhqk,nbkhd->nbqhd", p.astype(v.dtype), vb)
    o = o.reshape(N, nb * w, H, dh)[:, :L]
    lse = lse.transpose(0, 1, 3, 2).reshape(N, nb * w, H)[:, :L]
    return o, lse


def dilated_group_attention(q, k, v, window, dilation):
    B, S, H, dh = q.shape
    L = S // dilation
    w = window // (2 * dilation)

    def to_sub(t):
        return t.reshape(B, L, dilation, H, dh).transpose(0, 2, 1, 3, 4).reshape(B * dilation, L, H, dh)

    o, lse = band_attention(to_sub(q), to_sub(k), to_sub(v), w)
    o = o.reshape(B, dilation, L, H, dh).transpose(0, 2, 1, 3, 4).reshape(B, S, H, dh)
    lse = lse.reshape(B, dilation, L, H).transpose(0, 2, 1, 3).reshape(B, S, H)
    return o, lse


def dilated_mixer(qa, ka, va, q_norm_g, k_norm_g):
    B, S, _ = qa.shape
    pos = jnp.arange(S, dtype=jnp.float32)
    q = partial_rope(rmsnorm(qa.reshape(B, S, N_ATTN_HEADS, HEAD_DIM), q_norm_g), pos)
    k = partial_rope(rmsnorm(ka.reshape(B, S, N_ATTN_HEADS, HEAD_DIM), k_norm_g), pos)
    v = va.reshape(B, S, N_ATTN_HEADS, HEAD_DIM)
    outs, lses = [], []
    for gi, (window, dilation) in enumerate(ATTN_GROUPS):
        sl = slice(gi * HEADS_PER_GROUP, (gi + 1) * HEADS_PER_GROUP)
        o, l = dilated_group_attention(q[:, :, sl], k[:, :, sl], v[:, :, sl], window, dilation)
        outs.append(o)
        lses.append(l)
    alpha = jax.nn.softmax(jnp.stack(lses, axis=2), axis=2)
    o = jnp.stack(outs, axis=2) * alpha[..., None].astype(v.dtype)
    return o.reshape(B, S, ATTN_WIDTH)


def sgu_mixer(ub, vb, norm_g, w_s, b_s):
    B, S, _ = ub.shape
    u = jax.nn.gelu(ub)
    v = rmsnorm(jax.nn.gelu(vb), norm_g)
    vc = v.reshape(B, S // SGU_CHUNK, SGU_CHUNK, SGU_GROUPS, SGU_GROUP_WIDTH)
    mixed = jnp.einsum("gpq,bnqgc->bnpgc", w_s, vc) + b_s.T[None, None, :, :, None]
    return u * mixed.reshape(B, S, SGU_WIDTH)


def memory_cross_attention(qm, mem, mem_norm_g, w_mem_kv, q_norm_g, k_norm_g):
    B, S, _ = qm.shape
    M = mem.shape[1]
    q = rmsnorm(qm.reshape(B, S, MEM_HEADS, MEM_HEAD_DIM), q_norm_g)
    kv = rmsnorm(mem, mem_norm_g) @ w_mem_kv
    k, v = jnp.split(kv, 2, axis=-1)
    k = rmsnorm(k.reshape(B, M, MEM_HEADS, MEM_HEAD_DIM), k_norm_g)
    v = v.reshape(B, M, MEM_HEADS, MEM_HEAD_DIM)
    s = jnp.einsum("bshd,bmhd->bhsm", q, k).astype(jnp.float32) * (1.0 / math.sqrt(MEM_HEAD_DIM))
    p = jax.nn.softmax(s, axis=-1)
    o = jnp.einsum("bhsm,bmhd->bshd", p.astype(v.dtype), v)
    return o.reshape(B, S, MEM_WIDTH)


def expert_choice_moe(h, w_router, w_gate, w_up, w_down):
    B, S, D = h.shape
    n_tok = B * S
    cap = CAPACITY_FACTOR * n_tok // N_EXPERTS
    hf = h.reshape(n_tok, D)
    aff = jax.nn.softmax((hf @ w_router).astype(jnp.float32), axis=-1)
    gate, idx = lax.top_k(aff.T, cap)
    xe = hf[idx]
    a = jnp.einsum("ecd,edf->ecf", xe, w_gate)
    b = jnp.einsum("ecd,edf->ecf", xe, w_up)
    ye = jnp.einsum("ecf,efd->ecd", jax.nn.silu(a) * b, w_down) * gate[..., None].astype(h.dtype)
    out = jnp.zeros((n_tok, D), ye.dtype).at[idx.reshape(-1)].add(ye.reshape(-1, D))
    return out.reshape(B, S, D)


def encoder_layer(x, mem, norm1_g, w_in, q_norm_a, k_norm_a, sgu_norm_g, sgu_w, sgu_b,
                  mem_norm_g, w_mem_kv, q_norm_m, k_norm_m, w_proj_a, w_proj_b, w_proj_m,
                  w_out, norm2_g, w_router, w_gate, w_up, w_down):
    B, S, D = x.shape
    h = rmsnorm(x, norm1_g)
    z = h @ w_in
    qa, ka, va, ub, vb, qm, gl = jnp.split(z, SPLIT_POINTS, axis=-1)
    oa = dilated_mixer(qa, ka, va, q_norm_a, k_norm_a)
    ob = sgu_mixer(ub, vb, sgu_norm_g, sgu_w, sgu_b)
    om = memory_cross_attention(qm, mem, mem_norm_g, w_mem_kv, q_norm_m, k_norm_m)
    g = jax.nn.sigmoid(gl.astype(jnp.float32)).astype(x.dtype).reshape(B, S, N_BRANCHES, D)
    merged = (g[:, :, 0] * (oa @ w_proj_a) + g[:, :, 1] * (ob @ w_proj_b)
              + g[:, :, 2] * (om @ w_proj_m))
    x = x + merged @ w_out
    x = x + expert_choice_moe(rmsnorm(x, norm2_g), w_router, w_gate, w_up, w_down)
    return x


def setup_inputs(seed: int = 0) -> dict:
    key = jax.random.key(seed)
    ks = jax.random.split(key, 26)
    f32 = jnp.float32

    def nrm(k, shape, scale):
        return jax.random.normal(k, shape, f32) * scale

    def gain(k, shape):
        return 1.0 + 0.02 * jax.random.normal(k, shape, f32)

    L = DEPTH
    return {
        "x_prompt": nrm(ks[0], (BATCH, SEQ, D_MODEL), 1.0),
        "x_sample": nrm(ks[1], (DEC_BATCH, DEC_SEQ, D_MODEL), 1.0),
        "mem_prompt": nrm(ks[2], (BATCH, N_MEM, D_MODEL), 1.0),
        "mem_sample": nrm(ks[3], (DEC_BATCH, N_MEM, D_MODEL), 1.0),
        "norm1_g": gain(ks[4], (L, D_MODEL)),
        "w_in": nrm(ks[5], (L, D_MODEL, IN_WIDTH), D_MODEL ** -0.5),
        "q_norm_a": gain(ks[6], (L, HEAD_DIM)),
        "k_norm_a": gain(ks[7], (L, HEAD_DIM)),
        "sgu_norm_g": gain(ks[8], (L, SGU_WIDTH)),
        "sgu_w": nrm(ks[9], (L, SGU_GROUPS, SGU_CHUNK, SGU_CHUNK), SGU_CHUNK ** -0.5),
        "sgu_b": 1.0 + nrm(ks[10], (L, SGU_GROUPS, SGU_CHUNK), 0.02),
        "mem_norm_g": gain(ks[11], (L, D_MODEL)),
        "w_mem_kv": nrm(ks[12], (L, D_MODEL, 2 * MEM_WIDTH), D_MODEL ** -0.5),
        "q_norm_m": gain(ks[13], (L, MEM_HEAD_DIM)),
        "k_norm_m": gain(ks[14], (L, MEM_HEAD_DIM)),
        "w_proj_a": nrm(ks[15], (L, ATTN_WIDTH, D_MODEL), ATTN_WIDTH ** -0.5),
        "w_proj_b": nrm(ks[16], (L, SGU_WIDTH, D_MODEL), SGU_WIDTH ** -0.5),
        "w_proj_m": nrm(ks[17], (L, MEM_WIDTH, D_MODEL), MEM_WIDTH ** -0.5),
        "w_out": nrm(ks[18], (L, D_MODEL, D_MODEL), D_MODEL ** -0.5),
        "norm2_g": gain(ks[19], (L, D_MODEL)),
        "w_router": nrm(ks[20], (L, D_MODEL, N_EXPERTS), D_MODEL ** -0.5),
        "w_gate": nrm(ks[21], (L, N_EXPERTS, D_MODEL, EXPERT_FF), D_MODEL ** -0.5),
        "w_up": nrm(ks[22], (L, N_EXPERTS, D_MODEL, EXPERT_FF), D_MODEL ** -0.5),
        "w_down": nrm(ks[23], (L, N_EXPERTS, EXPERT_FF, D_MODEL), EXPERT_FF ** -0.5),
    }


def reference(x_prompt, x_sample, mem_prompt, mem_sample, norm1_g, w_in, q_norm_a, k_norm_a,
              sgu_norm_g, sgu_w, sgu_b, mem_norm_g, w_mem_kv, q_norm_m, k_norm_m,
              w_proj_a, w_proj_b, w_proj_m, w_out, norm2_g, w_router, w_gate, w_up, w_down):
    def run(x, mem):
        for l in range(DEPTH):
            x = encoder_layer(x, mem, norm1_g[l], w_in[l], q_norm_a[l], k_norm_a[l],
                              sgu_norm_g[l], sgu_w[l], sgu_b[l], mem_norm_g[l], w_mem_kv[l],
                              q_norm_m[l], k_norm_m[l], w_proj_a[l], w_proj_b[l], w_proj_m[l],
                              w_out[l], norm2_g[l], w_router[l], w_gate[l], w_up[l], w_down[l])
        return x

    y_prompt = run(x_prompt, mem_prompt)
    y_sample = run(x_sample, mem_sample)
    return (y_prompt, y_sample)
```

```python
import functools
import math

import numpy as np
import jax
import jax.numpy as jnp
from jax import lax
from jax.experimental import pallas as pl
from jax.experimental.pallas import tpu as pltpu

BF = jnp.bfloat16
F32 = jnp.float32

EPS = 1e-6
NEG = -1e30
LANES = 128
HEAD_DIM = 128
ATTN_GROUPS = ((128, 1), (512, 4), (2048, 16))
HEADS_PER_GROUP = 4
GROUP_WIDTH = HEADS_PER_GROUP * HEAD_DIM
ATTN_WIDTH = len(ATTN_GROUPS) * GROUP_WIDTH
BAND = 64
ROPE_THETA = 500000.0
ROPE_DIM = HEAD_DIM // 4
SGU_CHUNK = 128
SGU_GROUPS = 4
SGU_WIDTH = 1536
SGU_GROUP_WIDTH = SGU_WIDTH // SGU_GROUPS
MEM_HEADS = 4
MEM_HEAD_DIM = 256
MEM_WIDTH = MEM_HEADS * MEM_HEAD_DIM
CAPACITY_FACTOR = 2
ATTN_Q_BLOCK = 128
GATHER_CHUNK = 512
VMEM_LIMIT = 52 * 1024 * 1024


def _params(n_axes, vmem=VMEM_LIMIT):
    return pltpu.CompilerParams(dimension_semantics=("arbitrary",) * n_axes, vmem_limit_bytes=vmem)


def _rms(x, gain):
    ms = jnp.mean(x * x, axis=-1, keepdims=True)
    return x * lax.rsqrt(ms + EPS) * gain


def _gelu_tanh(x):
    c = math.sqrt(2.0 / math.pi)
    return x * (0.5 * (1.0 + jnp.tanh(c * (x + 0.044715 * (x * x * x)))))


def _cast_kernel(x_ref, o_ref):
    o_ref[...] = x_ref[...].astype(BF)


def _cast_bf16(w, rows=512):
    r, c = w.shape
    rows = min(rows, r)
    return pl.pallas_call(
        _cast_kernel,
        grid=(r // rows,),
        in_specs=[pl.BlockSpec((rows, c), lambda i: (i, 0))],
        out_specs=pl.BlockSpec((rows, c), lambda i: (i, 0)),
        out_shape=jax.ShapeDtypeStruct((r, c), BF),
        compiler_params=_params(1),
        name="cast_bf16",
    )(w)


def _qk_norm_rope(acc, gain, cos_t, sin_t):
    lane = lax.broadcasted_iota(jnp.int32, (1, HEAD_DIM), 1)
    half = ROPE_DIM // 2
    outs = []
    for h in range(HEADS_PER_GROUP):
        y = _rms(acc[:, h * HEAD_DIM:(h + 1) * HEAD_DIM], gain)
        partner = jnp.where(lane < half, pltpu.roll(y, HEAD_DIM - half, 1), pltpu.roll(y, half, 1))
        outs.append(y * cos_t + partner * sin_t)
    return jnp.concatenate(outs, axis=1)


def _store_dilated(val, scr, o_ref, d, tm):
    if d == 1:
        o_ref[0] = val.astype(BF)
        return
    for h in range(HEADS_PER_GROUP):
        scr[h] = val[:, h * HEAD_DIM:(h + 1) * HEAD_DIM]
    for r in range(d):
        o_ref[r] = jnp.concatenate(
            [scr[h, pl.ds(r, tm // d, stride=d), :] for h in range(HEADS_PER_GROUP)], axis=1).astype(BF)


def _inproj_attn_kernel(x_ref, g1_ref, w_ref, qkg_ref, cos_ref, sin_ref,
                        h_ref, o0_ref, o1_ref, o2_ref, h_scr, scr, *, tm):
    n = pl.program_id(1)

    @pl.when(n == 0)
    def _():
        h = _rms(x_ref[...], g1_ref[...]).astype(BF)
        h_scr[...] = h
        h_ref[...] = h

    acc = jnp.dot(h_scr[...], w_ref[...].astype(BF), preferred_element_type=F32)
    group = n // 3
    which = n % 3
    for g, (o_ref, (_, d)) in enumerate(zip((o0_ref, o1_ref, o2_ref), ATTN_GROUPS)):
        @pl.when((group == g) & (which < 2))
        def _(o_ref=o_ref, d=d):
            gain = qkg_ref[pl.ds(which, 1), :]
            _store_dilated(_qk_norm_rope(acc, gain, cos_ref[...], sin_ref[...]), scr, o_ref, d, tm)

        @pl.when((group == g) & (which == 2))
        def _(o_ref=o_ref, d=d):
            _store_dilated(acc, scr, o_ref, d, tm)


def _inproj_attn(x, g1, w_in, qk_gain, cos_t, sin_t, tm=1024):
    m_tok, dm = x.shape
    n_steps = 3 * len(ATTN_GROUPS)

    def out_spec(g, d):
        return pl.BlockSpec((None, d, tm // d, GROUP_WIDTH),
                            lambda m, n: (jnp.clip(n - 3 * g, 0, 2), 0, m, 0))

    outs = pl.pallas_call(
        functools.partial(_inproj_attn_kernel, tm=tm),
        grid=(m_tok // tm, n_steps),
        in_specs=[
            pl.BlockSpec((tm, dm), lambda m, n: (m, 0)),
            pl.BlockSpec((1, dm), lambda m, n: (0, 0)),
            pl.BlockSpec((dm, GROUP_WIDTH), lambda m, n: (0, (n % 3) * 3 + n // 3)),
            pl.BlockSpec((2, HEAD_DIM), lambda m, n: (0, 0)),
            pl.BlockSpec((tm, HEAD_DIM), lambda m, n: (m, 0)),
            pl.BlockSpec((tm, HEAD_DIM), lambda m, n: (m, 0)),
        ],
        out_specs=[pl.BlockSpec((tm, dm), lambda m, n: (m, 0))]
        + [out_spec(g, d) for g, (_, d) in enumerate(ATTN_GROUPS)],
        out_shape=[jax.ShapeDtypeStruct((m_tok, dm), BF)]
        + [jax.ShapeDtypeStruct((3, d, m_tok // d, GROUP_WIDTH), BF) for _, d in ATTN_GROUPS],
        scratch_shapes=[pltpu.VMEM((tm, dm), BF), pltpu.VMEM((HEADS_PER_GROUP, tm, HEAD_DIM), F32)],
        compiler_params=_params(2),
        name="inproj_attn",
    )(x, g1, w_in, qk_gain, cos_t, sin_t)
    return outs[0], outs[1:]


def _matmul_kernel(a_ref, w_ref, o_ref):
    o_ref[...] = jnp.dot(a_ref[...], w_ref[...].astype(BF),
                         preferred_element_type=F32).astype(o_ref.dtype)


def _inproj_rest(h, w_in, col0, tm=2048, tn=512):
    m_tok, dm = h.shape
    n_cols = w_in.shape[1] - col0
    tm = min(tm, m_tok)
    off = col0 // tn
    return pl.pallas_call(
        _matmul_kernel,
        grid=(m_tok // tm, n_cols // tn),
        in_specs=[pl.BlockSpec((tm, dm), lambda m, n: (m, 0)),
                  pl.BlockSpec((dm, tn), lambda m, n: (0, off + n))],
        out_specs=pl.BlockSpec((tm, tn), lambda m, n: (m, n)),
        out_shape=jax.ShapeDtypeStruct((m_tok, n_cols), BF),
        compiler_params=_params(2),
        name="inproj_rest",
    )(h, w_in)


def _attn_kernel(q_ref, kp_ref, kc_ref, kn_ref, vp_ref, vc_ref, vn_ref, o_ref, lse_ref, *, bounds):
    tq = ATTN_Q_BLOCK
    j0 = pl.program_id(1) * tq
    lo = jnp.int32(bounds[0])
    hi = jnp.int32(bounds[1])
    for b0, b1 in zip(bounds[1:-1], bounds[2:]):
        lo = jnp.where(j0 >= b0, b0, lo)
        hi = jnp.where(j0 >= b0, b1, hi)
    k = jnp.concatenate([kp_ref[...], kc_ref[...], kn_ref[...]], axis=0)
    v = jnp.concatenate([vp_ref[...], vc_ref[...], vn_ref[...]], axis=0)
    q = q_ref[...]
    qpos = j0 + lax.broadcasted_iota(jnp.int32, (tq, 1), 0)
    kpos = j0 - tq + lax.broadcasted_iota(jnp.int32, (1, 3 * tq), 1)
    rel = kpos - qpos
    valid = (jnp.abs(rel) <= BAND) & (kpos >= lo) & (kpos < hi)
    scale = 1.0 / math.sqrt(HEAD_DIM)
    lane = lax.broadcasted_iota(jnp.int32, (1, LANES), 1)
    lse_all = jnp.zeros((tq, LANES), F32)
    for h in range(HEADS_PER_GROUP):
        cols = slice(h * HEAD_DIM, (h + 1) * HEAD_DIM)
        s = lax.dot_general(q[:, cols], k[:, cols], (((1,), (1,)), ((), ())),
                            preferred_element_type=F32) * scale
        s = jnp.where(valid, s, NEG)
        mx = jnp.max(s, axis=-1, keepdims=True)
        p = jnp.exp(s - mx)
        den = jnp.sum(p, axis=-1, keepdims=True)
        o = jnp.dot(p.astype(BF), v[:, cols], preferred_element_type=F32)
        o_ref[:, cols] = (o / den).astype(BF)
        lse_all = jnp.where(lane == h, mx + jnp.log(den), lse_all)
    lse_ref[...] = lse_all


def _band_attention(qkv, d, seq_bounds):
    _, _, n_j, _ = qkv.shape
    tq = ATTN_Q_BLOCK
    n_blocks = n_j // tq
    bounds = tuple(b // d for b in seq_bounds)

    def spec(which, shift):
        return pl.BlockSpec((None, None, tq, GROUP_WIDTH),
                            lambda r, i: (which, r, jnp.clip(i + shift, 0, n_blocks - 1), 0))

    return pl.pallas_call(
        functools.partial(_attn_kernel, bounds=bounds),
        grid=(d, n_blocks),
        in_specs=[spec(0, 0), spec(1, -1), spec(1, 0), spec(1, 1), spec(2, -1), spec(2, 0), spec(2, 1)],
        out_specs=[pl.BlockSpec((None, tq, GROUP_WIDTH), lambda r, i: (r, i, 0)),
                   pl.BlockSpec((None, tq, LANES), lambda r, i: (r, i, 0))],
        out_shape=[jax.ShapeDtypeStruct((d, n_j, GROUP_WIDTH), BF),
                   jax.ShapeDtypeStruct((d, n_j, LANES), F32)],
        compiler_params=_params(2),
        name=f"band_attention_d{d}",
    )(qkv, qkv, qkv, qkv, qkv, qkv, qkv)


def _sgu_kernel(u_ref, v_ref, ng_ref, ws_ref, bs_ref, o_ref, *, tb):
    vn = _rms(_gelu_tanh(v_ref[...].astype(F32)), ng_ref[...]).astype(BF)
    for g in range(SGU_GROUPS):
        cols = slice(g * SGU_GROUP_WIDTH, (g + 1) * SGU_GROUP_WIDTH)
        w = ws_ref[g].astype(BF)
        bias = bs_ref[:, g:g + 1]
        for c in range(tb // SGU_CHUNK):
            rows = slice(c * SGU_CHUNK, (c + 1) * SGU_CHUNK)
            mixed = jnp.dot(w, vn[rows, cols], preferred_element_type=F32) + bias
            u = _gelu_tanh(u_ref[rows, cols].astype(F32))
            o_ref[rows, cols] = (u * mixed).astype(BF)


def _sgu(z_rest, norm_g, w_s, b_s_t, tb=512):
    m_tok = z_rest.shape[0]
    return pl.pallas_call(
        functools.partial(_sgu_kernel, tb=tb),
        grid=(m_tok // tb,),
        in_specs=[pl.BlockSpec((tb, SGU_WIDTH), lambda i: (i, 0)),
                  pl.BlockSpec((tb, SGU_WIDTH), lambda i: (i, 1)),
                  pl.BlockSpec((1, SGU_WIDTH), lambda i: (0, 0)),
                  pl.BlockSpec((SGU_GROUPS, SGU_CHUNK, SGU_CHUNK), lambda i: (0, 0, 0)),
                  pl.BlockSpec((SGU_CHUNK, SGU_GROUPS), lambda i: (0, 0))],
        out_specs=pl.BlockSpec((tb, SGU_WIDTH), lambda i: (i, 0)),
        out_shape=jax.ShapeDtypeStruct((m_tok, SGU_WIDTH), BF),
        compiler_params=_params(1),
        name="sgu",
    )(z_rest, z_rest, norm_g, w_s, b_s_t)


def _mem_kv_kernel(mem_ref, ng_ref, w_ref, kg_ref, o_ref, *, n_k_tiles, tn):
    n = pl.program_id(1)
    hm = _rms(mem_ref[...], ng_ref[...]).astype(BF)
    kv = jnp.dot(hm, w_ref[...].astype(BF), preferred_element_type=F32)

    @pl.when(n < n_k_tiles)
    def _():
        for h in range(tn // MEM_HEAD_DIM):
            cols = slice(h * MEM_HEAD_DIM, (h + 1) * MEM_HEAD_DIM)
            o_ref[:, cols] = _rms(kv[:, cols], kg_ref[...]).astype(BF)

    @pl.when(n >= n_k_tiles)
    def _():
        o_ref[...] = kv.astype(BF)


def _mem_kv(mem, norm_g, w_kv, k_gain, tn=512):
    n_b, n_mem, dm = mem.shape
    width = w_kv.shape[1]
    return pl.pallas_call(
        functools.partial(_mem_kv_kernel, n_k_tiles=MEM_WIDTH // tn, tn=tn),
        grid=(n_b, width // tn),
        in_specs=[pl.BlockSpec((None, n_mem, dm), lambda b, n: (b, 0, 0)),
                  pl.BlockSpec((1, dm), lambda b, n: (0, 0)),
                  pl.BlockSpec((dm, tn), lambda b, n: (0, n)),
                  pl.BlockSpec((1, MEM_HEAD_DIM), lambda b, n: (0, 0))],
        out_specs=pl.BlockSpec((None, n_mem, tn), lambda b, n: (b, 0, n)),
        out_shape=jax.ShapeDtypeStruct((n_b, n_mem, width), BF),
        compiler_params=_params(2),
        name="mem_kv",
    )(mem, norm_g, w_kv, k_gain)


def _mem_attn_kernel(q_ref, kv_ref, qg_ref, o_ref):
    scale = 1.0 / math.sqrt(MEM_HEAD_DIM)
    for h in range(MEM_HEADS):
        cols = slice(h * MEM_HEAD_DIM, (h + 1) * MEM_HEAD_DIM)
        q = _rms(q_ref[:, cols].astype(F32), qg_ref[...]).astype(BF)
        k = kv_ref[:, cols]
        v = kv_ref[:, MEM_WIDTH + h * MEM_HEAD_DIM:MEM_WIDTH + (h + 1) * MEM_HEAD_DIM]
        s = lax.dot_general(q, k, (((1,), (1,)), ((), ())), preferred_element_type=F32) * scale
        mx = jnp.max(s, axis=-1, keepdims=True)
        p = jnp.exp(s - mx)
        den = jnp.sum(p, axis=-1, keepdims=True)
        o = jnp.dot(p.astype(BF), v, preferred_element_type=F32)
        o_ref[:, cols] = (o / den).astype(BF)


def _mem_attn(z_rest, kv, q_gain, batch_bounds, tm=512):
    m_tok = z_rest.shape[0]
    n_mem = kv.shape[1]
    q_block = (2 * SGU_WIDTH) // MEM_WIDTH

    def batch_of(i):
        b = 0
        for bound in batch_bounds[1:-1]:
            b = b + (i * tm >= bound).astype(jnp.int32)
        return b

    return pl.pallas_call(
        _mem_attn_kernel,
        grid=(m_tok // tm,),
        in_specs=[pl.BlockSpec((tm, MEM_WIDTH), lambda i: (i, q_block)),
                  pl.BlockSpec((None, n_mem, 2 * MEM_WIDTH), lambda i: (batch_of(i), 0, 0)),
                  pl.BlockSpec((1, MEM_HEAD_DIM), lambda i: (0, 0))],
        out_specs=pl.BlockSpec((tm, MEM_WIDTH), lambda i: (i, 0)),
        out_shape=jax.ShapeDtypeStruct((m_tok, MEM_WIDTH), BF),
        compiler_params=_params(1),
        name="mem_attn",
    )(z_rest, kv, q_gain)


def _merge_kernel(oa0_ref, oa1_ref, oa2_ref, l0_ref, l1_ref, l2_ref, ob_ref, om_ref,
                  g0_ref, g1_ref, g2_ref, wa_ref, wb_ref, wm_ref, o_ref, oa_scr, lse_scr, *, tm):
    n_h = HEADS_PER_GROUP
    n_g = len(ATTN_GROUPS)
    for g, (oa_ref, l_ref, (_, d)) in enumerate(zip((oa0_ref, oa1_ref, oa2_ref),
                                                    (l0_ref, l1_ref, l2_ref), ATTN_GROUPS)):
        for r in range(d):
            rows = pl.ds(r, tm // d, stride=d) if d > 1 else slice(None)
            o = oa_ref[r].astype(F32)
            for h in range(n_h):
                oa_scr[g * n_h + h, rows, :] = o[:, h * HEAD_DIM:(h + 1) * HEAD_DIM]
            lse_scr[g, rows, :] = l_ref[r]
    lses = [lse_scr[g] for g in range(n_g)]
    mx = jnp.maximum(jnp.maximum(lses[0], lses[1]), lses[2])
    es = [jnp.exp(l - mx) for l in lses]
    den = es[0] + es[1] + es[2]
    slabs = []
    for g in range(n_g):
        alpha = es[g] / den
        slabs += [(oa_scr[g * n_h + h] * alpha[:, h:h + 1]).astype(BF) for h in range(n_h)]
    pa = jnp.dot(jnp.concatenate(slabs, axis=1), wa_ref[...], preferred_element_type=F32)
    merged = jax.nn.sigmoid(g0_ref[...].astype(F32)) * pa
    pb = jnp.dot(ob_ref[...], wb_ref[...], preferred_element_type=F32)
    merged = merged + jax.nn.sigmoid(g1_ref[...].astype(F32)) * pb
    pm = jnp.dot(om_ref[...], wm_ref[...], preferred_element_type=F32)
    merged = merged + jax.nn.sigmoid(g2_ref[...].astype(F32)) * pm
    o_ref[...] = merged.astype(BF)


def _resident(shape):
    return pl.BlockSpec(shape, lambda i: (0,) * len(shape), pipeline_mode=pl.Buffered(1))


def _merge(oas, lses, ob, om, z_rest, wa, wb, wm, dm, tm=512):
    m_tok = ob.shape[0]
    gate0 = (2 * SGU_WIDTH + MEM_WIDTH) // dm
    oa_specs = [pl.BlockSpec((d, tm // d, GROUP_WIDTH), lambda i: (0, i, 0)) for _, d in ATTN_GROUPS]
    l_specs = [pl.BlockSpec((d, tm // d, LANES), lambda i: (0, i, 0)) for _, d in ATTN_GROUPS]
    g_specs = [pl.BlockSpec((tm, dm), lambda i, b=b: (i, gate0 + b)) for b in range(3)]
    return pl.pallas_call(
        functools.partial(_merge_kernel, tm=tm),
        grid=(m_tok // tm,),
        in_specs=oa_specs + l_specs
        + [pl.BlockSpec((tm, SGU_WIDTH), lambda i: (i, 0)), pl.BlockSpec((tm, MEM_WIDTH), lambda i: (i, 0))]
        + g_specs + [_resident(wa.shape), _resident(wb.shape), _resident(wm.shape)],
        out_specs=pl.BlockSpec((tm, dm), lambda i: (i, 0)),
        out_shape=jax.ShapeDtypeStruct((m_tok, dm), BF),
        scratch_shapes=[pltpu.VMEM((len(ATTN_GROUPS) * HEADS_PER_GROUP, tm, HEAD_DIM), F32),
                        pltpu.VMEM((len(ATTN_GROUPS), tm, LANES), F32)],
        compiler_params=_params(1),
        name="merge",
    )(*oas, *lses, ob, om, z_rest, z_rest, z_rest, wa, wb, wm)


def _out_kernel(mg_ref, x_ref, wo_ref, g2_ref, wr_ref, xp_ref, xs_ref, h2_ref, lg_ref, *, n_prompt_tiles):
    i = pl.program_id(0)
    x2 = x_ref[...] + jnp.dot(mg_ref[...], wo_ref[...], preferred_element_type=F32)

    @pl.when(i < n_prompt_tiles)
    def _():
        xp_ref[...] = x2

    @pl.when(i >= n_prompt_tiles)
    def _():
        xs_ref[...] = x2

    h2 = _rms(x2, g2_ref[...])
    h2_ref[...] = h2
    h_hi = h2.astype(BF)
    h_lo = (h2 - h_hi.astype(F32)).astype(BF)
    wr = wr_ref[...]
    w_hi = wr.astype(BF)
    w_lo = (wr - w_hi.astype(F32)).astype(BF)
    nt = (((1,), (1,)), ((), ()))
    lg = lax.dot_general(w_hi, h_hi, nt, preferred_element_type=F32)
    lg = lg + lax.dot_general(w_hi, h_lo, nt, preferred_element_type=F32)
    lg = lg + lax.dot_general(w_lo, h_hi, nt, preferred_element_type=F32)
    lg_ref[...] = lg


def _out_proj(merged, x, wo, g2, w_router_t, n_prompt, tm=512):
    m_tok, dm = x.shape
    n_e = w_router_t.shape[0]
    npt = n_prompt // tm
    return pl.pallas_call(
        functools.partial(_out_kernel, n_prompt_tiles=npt),
        grid=(m_tok // tm,),
        in_specs=[pl.BlockSpec((tm, dm), lambda i: (i, 0)),
                  pl.BlockSpec((tm, dm), lambda i: (i, 0)),
                  _resident(wo.shape),
                  pl.BlockSpec((1, dm), lambda i: (0, 0)),
                  pl.BlockSpec((n_e, dm), lambda i: (0, 0))],
        out_specs=[pl.BlockSpec((tm, dm), lambda i: (jnp.minimum(i, npt - 1), 0)),
                   pl.BlockSpec((tm, dm), lambda i: (jnp.maximum(i - npt, 0), 0)),
                   pl.BlockSpec((tm, dm), lambda i: (i, 0)),
                   pl.BlockSpec((n_e, tm), lambda i: (0, i))],
        out_shape=[jax.ShapeDtypeStruct((n_prompt, dm), F32),
                   jax.ShapeDtypeStruct((m_tok - n_prompt, dm), F32),
                   jax.ShapeDtypeStruct((m_tok, dm), F32),
                   jax.ShapeDtypeStruct((n_e, m_tok), F32)],
        compiler_params=_params(1),
        name="out_proj",
    )(merged, x, wo, g2, w_router_t)


def _router_kernel(lg_ref, u_ref, bl_ref, ui_ref, idx_ref, gate_ref, *, cap):
    n_e, n_r, _ = lg_ref.shape
    lg = lg_ref[...]
    ex = jnp.exp(lg - jnp.max(lg, axis=0, keepdims=True))
    aff = ex / jnp.sum(ex, axis=0, keepdims=True)
    key = lax.bitcast_convert_type(aff, jnp.int32)

    def count(mask):
        return jnp.sum(jnp.sum(mask.astype(F32), axis=2, keepdims=True), axis=1, keepdims=True)

    thr = jnp.zeros((n_e, 1, 1), jnp.int32)
    for bit in range(30, -1, -1):
        cand = thr | jnp.int32(1 << bit)
        thr = jnp.where(count(key >= cand) >= cap, cand, thr)
    gt = key > thr
    eq = key == thr
    need = cap - count(gt)

    def prefix(mask_f32):
        m2 = mask_f32.reshape(n_e * n_r, LANES)
        incl = jnp.dot(m2.astype(BF), u_ref[...], preferred_element_type=F32)
        tot = jnp.broadcast_to(incl[:, LANES - 1:LANES], incl.shape)
        rowoff = jnp.dot(bl_ref[...], tot.astype(BF), preferred_element_type=F32)
        return rowoff + incl - m2, incl

    eq_rank, _ = prefix(eq.astype(F32))
    take_eq = eq & (eq_rank.reshape(n_e, n_r, LANES) < need)
    sel = jnp.where(gt, 1.0, jnp.where(take_eq, 1.0, 0.0))
    _, incl = prefix(sel)
    sel2 = sel.reshape(n_e * n_r, LANES).astype(BF)

    ones8 = jnp.ones((8, LANES), BF)
    slot = lax.broadcasted_iota(jnp.int32, (cap, 1), 0).astype(F32)
    row_id = lax.broadcasted_iota(jnp.int32, (cap, n_r), 1).astype(F32)
    lane_id = lax.broadcasted_iota(jnp.int32, (cap, LANES), 1).astype(F32)
    nt = (((1,), (1,)), ((), ()))
    for e in range(n_e):
        rows = slice(e * n_r, (e + 1) * n_r)
        tot_row = lax.dot_general(ones8, sel2[rows], nt, preferred_element_type=F32)[0:1]
        cum_row = jnp.dot(tot_row.astype(BF), ui_ref[...], preferred_element_type=F32)
        r_s = jnp.sum((cum_row <= slot).astype(F32), axis=1, keepdims=True)
        onehot = row_id == r_s
        oh = jnp.where(onehot, 1.0, 0.0).astype(BF)
        row_start = jnp.sum(jnp.where(onehot, cum_row - tot_row, 0.0), axis=1, keepdims=True)
        g_incl = jnp.dot(oh, incl[rows].astype(BF), preferred_element_type=F32)
        l_s = jnp.sum((g_incl <= slot - row_start).astype(F32), axis=1, keepdims=True)
        idx_ref[e] = (r_s * LANES + l_s).astype(jnp.int32)
        a = aff[e]
        a1 = a.astype(BF)
        a2 = (a - a1.astype(F32)).astype(BF)
        a3 = (a - a1.astype(F32) - a2.astype(F32)).astype(BF)
        g_aff = (jnp.dot(oh, a1, preferred_element_type=F32)
                 + jnp.dot(oh, a2, preferred_element_type=F32)) + jnp.dot(oh, a3, preferred_element_type=F32)
        gate_ref[e] = jnp.sum(jnp.where(lane_id == l_s, g_aff, 0.0), axis=1, keepdims=True)


def _route(logits, cap):
    n_g, n_e, n_r, _ = logits.shape
    a = np.arange(LANES)
    upper = jnp.asarray(a[:, None] <= a[None, :], BF)
    i = np.arange(n_e * n_r)
    block_lower = jnp.asarray((i[:, None] // n_r == i[None, :] // n_r) & (i[None, :] < i[:, None]), BF)
    r = np.arange(n_r)
    upper_r = jnp.asarray(r[:, None] <= r[None, :], BF)
    return pl.pallas_call(
        functools.partial(_router_kernel, cap=cap),
        grid=(n_g,),
        in_specs=[pl.BlockSpec((None, n_e, n_r, LANES), lambda g: (g, 0, 0, 0)),
                  pl.BlockSpec((LANES, LANES), lambda g: (0, 0)),
                  pl.BlockSpec((n_e * n_r, n_e * n_r), lambda g: (0, 0)),
                  pl.BlockSpec((n_r, n_r), lambda g: (0, 0))],
        out_specs=[pl.BlockSpec((None, n_e, cap, 1), lambda g: (g, 0, 0, 0)),
                   pl.BlockSpec((None, n_e, cap, 1), lambda g: (g, 0, 0, 0))],
        out_shape=[jax.ShapeDtypeStruct((n_g, n_e, cap, 1), jnp.int32),
                   jax.ShapeDtypeStruct((n_g, n_e, cap, 1), F32)],
        compiler_params=_params(1),
        name="router",
    )(logits, upper, block_lower, upper_r)


def _expert_kernel(idx_ref, h2_hbm, gate_ref, wg_ref, wu_ref, wd_ref, xp_in, xs_in,
                   yp_hbm, ys_hbm, xb, stage, acc, sem_in, sem_out, *, cap, n_group_tok, chunk):
    del xp_in, xs_in
    e = pl.program_id(0)
    f = pl.program_id(1)
    n_f = pl.num_programs(1)
    n_slots = 2 * cap
    n_chunks = n_slots // chunk

    def row_copies(hbm, base, tok_off, to_vmem, sem):
        def body(s, carry):
            t = idx_ref[base + s] - tok_off
            src, dst = hbm.at[pl.ds(t, 1)], stage.at[pl.ds(s, 1)]
            if not to_vmem:
                src, dst = dst, src
            pltpu.make_async_copy(src, dst, sem).start()
            return carry
        lax.fori_loop(0, chunk, body, 0)

    def wait_rows(hbm, to_vmem, sem):
        src, dst = hbm.at[pl.ds(0, chunk)], stage
        if not to_vmem:
            src, dst = dst, src
        pltpu.make_async_copy(src, dst, sem).wait()

    @pl.when(f == 0)
    def _():
        for c in range(n_chunks):
            row_copies(h2_hbm, e * n_slots + c * chunk, 0, True, sem_in)
            wait_rows(h2_hbm, True, sem_in)
            xb[c * chunk:(c + 1) * chunk, :] = stage[...].astype(BF)

    wg = wg_ref[...].astype(BF)
    wu = wu_ref[...].astype(BF)
    wd = wd_ref[...].astype(BF)
    for c in range(n_chunks):
        rows = slice(c * chunk, (c + 1) * chunk)
        x = xb[rows, :]
        a = jnp.dot(x, wg, preferred_element_type=F32)
        b = jnp.dot(x, wu, preferred_element_type=F32)
        hmid = (a * jax.nn.sigmoid(a) * b).astype(BF)
        y = jnp.dot(hmid, wd, preferred_element_type=F32)

        @pl.when(f == 0)
        def _(rows=rows, y=y):
            acc[rows, :] = y

        @pl.when(f > 0)
        def _(rows=rows, y=y):
            acc[rows, :] += y

    @pl.when(f == n_f - 1)
    def _():
        for c in range(n_chunks):
            group = (c * chunk) // cap
            y_hbm = (yp_hbm, ys_hbm)[group]
            base = e * n_slots + c * chunk
            tok_off = group * n_group_tok
            rows = slice(c * chunk, (c + 1) * chunk)
            row_copies(y_hbm, base, tok_off, True, sem_in)
            wait_rows(y_hbm, True, sem_in)
            stage[...] = stage[...] + acc[rows, :] * gate_ref[rows, :]
            row_copies(y_hbm, base, tok_off, False, sem_out)
            wait_rows(y_hbm, False, sem_out)


def _experts(idx_flat, h2, gate, w_gate, w_up, w_down, x2p, x2s, cap, tf=256):
    n_e, dm, ff = w_gate.shape
    n_slots = 2 * cap
    chunk = min(GATHER_CHUNK, cap)
    n_group_tok = x2p.shape[0]
    any_spec = pl.BlockSpec(memory_space=pl.ANY)
    grid_spec = pltpu.PrefetchScalarGridSpec(
        num_scalar_prefetch=1,
        grid=(n_e, ff // tf),
        in_specs=[any_spec,
                  pl.BlockSpec((None, n_slots, 1), lambda e, f, idx: (e, 0, 0)),
                  pl.BlockSpec((None, dm, tf), lambda e, f, idx: (e, 0, f)),
                  pl.BlockSpec((None, dm, tf), lambda e, f, idx: (e, 0, f)),
                  pl.BlockSpec((None, tf, dm), lambda e, f, idx: (e, f, 0)),
                  any_spec, any_spec],
        out_specs=[any_spec, any_spec],
        scratch_shapes=[pltpu.VMEM((n_slots, dm), BF),
                        pltpu.VMEM((chunk, dm), F32),
                        pltpu.VMEM((n_slots, dm), F32),
                        pltpu.SemaphoreType.DMA(()),
                        pltpu.SemaphoreType.DMA(())],
    )
    return pl.pallas_call(
        functools.partial(_expert_kernel, cap=cap, n_group_tok=n_group_tok, chunk=chunk),
        grid_spec=grid_spec,
        out_shape=[jax.ShapeDtypeStruct(x2p.shape, F32), jax.ShapeDtypeStruct(x2s.shape, F32)],
        input_output_aliases={6: 0, 7: 1},
        compiler_params=_params(2),
        name="experts",
    )(idx_flat, h2, gate, w_gate, w_up, w_down, x2p, x2s)


def _rope_tables(seq_lens):
    half = ROPE_DIM // 2
    inv = ROPE_THETA ** (-jnp.arange(0, ROPE_DIM, 2, dtype=F32) / ROPE_DIM)
    pos = jnp.concatenate([jnp.arange(s, dtype=F32) for s in seq_lens])
    ang = pos[:, None] * inv[None, :]
    cos, sin = jnp.cos(ang), jnp.sin(ang)
    n = pos.shape[0]
    cos_t = jnp.concatenate([cos, cos, jnp.ones((n, HEAD_DIM - 2 * half), F32)], axis=1)
    sin_t = jnp.concatenate([-sin, sin, jnp.zeros((n, HEAD_DIM - 2 * half), F32)], axis=1)
    return cos_t, sin_t


def kernel(x_prompt, x_sample, mem_prompt, mem_sample, norm1_g, w_in, q_norm_a, k_norm_a, sgu_norm_g,
           sgu_w, sgu_b, mem_norm_g, w_mem_kv, q_norm_m, k_norm_m, w_proj_a, w_proj_b, w_proj_m, w_out,
           norm2_g, w_router, w_gate, w_up, w_down):
    assert norm1_g.shape[0] == 1, "one layer"
    bp, sp, dm = x_prompt.shape
    bs, ss, _ = x_sample.shape
    n_prompt, n_sample = bp * sp, bs * ss
    assert n_prompt == n_sample, "both request groups route the same number of tokens"
    n_e = w_router.shape[-1]
    cap = CAPACITY_FACTOR * n_prompt // n_e
    seq_lens = (sp,) * bp + (ss,) * bs
    seq_bounds = tuple(int(v) for v in np.concatenate([[0], np.cumsum(seq_lens)]))

    x = jnp.concatenate([x_prompt.reshape(n_prompt, dm), x_sample.reshape(n_sample, dm)], axis=0)
    mem = jnp.concatenate([mem_prompt, mem_sample], axis=0)
    cos_t, sin_t = _rope_tables(seq_lens)
    qk_gain = jnp.concatenate([q_norm_a, k_norm_a], axis=0)

    def layer0(w):
        return w.reshape(w.shape[1:])

    w_in, sgu_w, sgu_b, w_mem_kv, w_proj_a, w_proj_b, w_proj_m, w_out, w_router, w_gate, w_up, w_down = map(
        layer0, (w_in, sgu_w, sgu_b, w_mem_kv, w_proj_a, w_proj_b, w_proj_m, w_out, w_router, w_gate, w_up,
                 w_down))

    h, qkvs = _inproj_attn(x, norm1_g, w_in, qk_gain, cos_t, sin_t)
    z_rest = _inproj_rest(h, w_in, 3 * ATTN_WIDTH)

    oas, lses = [], []
    for qkv, (_, d) in zip(qkvs, ATTN_GROUPS):
        o, lse = _band_attention(qkv, d, seq_bounds)
        oas.append(o)
        lses.append(lse)
    ob = _sgu(z_rest, sgu_norm_g, sgu_w, sgu_b.T)
    kv = _mem_kv(mem, mem_norm_g, w_mem_kv, k_norm_m)
    om = _mem_attn(z_rest, kv, q_norm_m, seq_bounds)

    merged = _merge(oas, lses, ob, om, z_rest, _cast_bf16(w_proj_a), _cast_bf16(w_proj_b),
                    _cast_bf16(w_proj_m), dm)
    x2p, x2s, h2, logits_t = _out_proj(merged, x, _cast_bf16(w_out), norm2_g, w_router.T, n_prompt)

    logits = logits_t.reshape(n_e, 2, n_prompt // LANES, LANES).transpose(1, 0, 2, 3)
    idx, gate = _route(logits, cap)
    tok_off = jnp.arange(2, dtype=jnp.int32)[:, None, None] * n_prompt
    idx_flat = (idx[..., 0] + tok_off).transpose(1, 0, 2).reshape(-1)
    gate = gate.transpose(1, 0, 2, 3).reshape(n_e, 2 * cap, 1)

    yp, ys = _experts(idx_flat, h2, gate, w_gate, w_up, w_down, x2p, x2s, cap)
    return yp.reshape(bp, sp, dm), ys.reshape(bs, ss, dm)
```

```python
import functools
import math

import numpy as np
import jax
import jax.numpy as jnp
from jax import lax
from jax.experimental import pallas as pl
from jax.experimental.pallas import tpu as pltpu

BF = jnp.bfloat16
F32 = jnp.float32

EPS = 1e-6
NEG = -1e30
LANES = 128
HEAD_DIM = 128
ATTN_GROUPS = ((128, 1), (512, 4), (2048, 16))
HEADS_PER_GROUP = 4
GROUP_WIDTH = HEADS_PER_GROUP * HEAD_DIM
ATTN_WIDTH = len(ATTN_GROUPS) * GROUP_WIDTH
BAND = 64
ROPE_THETA = 500000.0
ROPE_DIM = HEAD_DIM // 4
SGU_CHUNK = 128
SGU_GROUPS = 4
SGU_WIDTH = 1536
SGU_GROUP_WIDTH = SGU_WIDTH // SGU_GROUPS
MEM_HEADS = 4
MEM_HEAD_DIM = 256
MEM_WIDTH = MEM_HEADS * MEM_HEAD_DIM
CAPACITY_FACTOR = 2
ATTN_SUB_BLOCK = 128
ATTN_Q_BLOCK = 256
GATHER_CHUNK = 512
ROW_DMA_UNROLL = 8
DOWN_TILE = 256
VMEM_LIMIT = 52 * 1024 * 1024


def _params(n_axes, vmem=VMEM_LIMIT):
    return pltpu.CompilerParams(dimension_semantics=("arbitrary",) * n_axes, vmem_limit_bytes=vmem)


def _rms(x, gain):
    ms = jnp.mean(x * x, axis=-1, keepdims=True)
    return x * lax.rsqrt(ms + EPS) * gain


def _gelu_tanh(x):
    c = math.sqrt(2.0 / math.pi)
    return x * (0.5 * (1.0 + jnp.tanh(c * (x + 0.044715 * (x * x * x)))))


def _cast_kernel(x_ref, o_ref):
    o_ref[...] = x_ref[...].astype(BF)


def _cast_bf16(w, rows=512):
    r, c = w.shape
    rows = min(rows, r)
    return pl.pallas_call(
        _cast_kernel,
        grid=(r // rows,),
        in_specs=[pl.BlockSpec((rows, c), lambda i: (i, 0))],
        out_specs=pl.BlockSpec((rows, c), lambda i: (i, 0)),
        out_shape=jax.ShapeDtypeStruct((r, c), BF),
        compiler_params=_params(1),
        name="cast_bf16",
    )(w)


def _qk_norm_rope(acc, gain, cos_t, sin_t):
    lane = lax.broadcasted_iota(jnp.int32, (1, HEAD_DIM), 1)
    half = ROPE_DIM // 2
    outs = []
    for h in range(HEADS_PER_GROUP):
        y = _rms(acc[:, h * HEAD_DIM:(h + 1) * HEAD_DIM], gain)
        partner = jnp.where(lane < half, pltpu.roll(y, HEAD_DIM - half, 1), pltpu.roll(y, half, 1))
        outs.append(y * cos_t + partner * sin_t)
    return jnp.concatenate(outs, axis=1)


def _store_dilated(val, scr, o_ref, d, tm):
    if d == 1:
        o_ref[0] = val.astype(BF)
        return
    for h in range(HEADS_PER_GROUP):
        scr[h] = val[:, h * HEAD_DIM:(h + 1) * HEAD_DIM]
    for r in range(d):
        o_ref[r] = jnp.concatenate(
            [scr[h, pl.ds(r, tm // d, stride=d), :] for h in range(HEADS_PER_GROUP)], axis=1).astype(BF)


def _inproj_attn_kernel(xp_ref, xs_ref, g1_ref, w_ref, qkg_ref, cos_ref, sin_ref,
                        h_ref, o0_ref, o1_ref, o2_ref, h_scr, scr, *, tm, n_prompt_tiles):
    m = pl.program_id(0)
    n = pl.program_id(1)

    for x_ref, is_mine in ((xp_ref, m < n_prompt_tiles), (xs_ref, m >= n_prompt_tiles)):
        @pl.when((n == 0) & is_mine)
        def _(x_ref=x_ref):
            h = _rms(x_ref[...], g1_ref[...]).astype(BF)
            h_scr[...] = h
            h_ref[...] = h

    acc = jnp.dot(h_scr[...], w_ref[...].astype(BF), preferred_element_type=F32)
    group = n // 3
    which = n % 3
    for g, (o_ref, (_, d)) in enumerate(zip((o0_ref, o1_ref, o2_ref), ATTN_GROUPS)):
        @pl.when((group == g) & (which < 2))
        def _(o_ref=o_ref, d=d):
            gain = qkg_ref[pl.ds(which, 1), :]
            _store_dilated(_qk_norm_rope(acc, gain, cos_ref[...], sin_ref[...]), scr, o_ref, d, tm)

        @pl.when((group == g) & (which == 2))
        def _(o_ref=o_ref, d=d):
            _store_dilated(acc, scr, o_ref, d, tm)


def _inproj_attn(xp, xs, g1, w_in, qk_gain, cos_t, sin_t, tm=1024):
    dm = xp.shape[1]
    m_tok = xp.shape[0] + xs.shape[0]
    npt = xp.shape[0] // tm
    n_steps = 3 * len(ATTN_GROUPS)

    def out_spec(g, d):
        return pl.BlockSpec((None, d, tm // d, GROUP_WIDTH),
                            lambda m, n: (jnp.clip(n - 3 * g, 0, 2), 0, m, 0))

    outs = pl.pallas_call(
        functools.partial(_inproj_attn_kernel, tm=tm, n_prompt_tiles=npt),
        grid=(m_tok // tm, n_steps),
        in_specs=[
            pl.BlockSpec((tm, dm), lambda m, n: (jnp.minimum(m, npt - 1), 0), pipeline_mode=pl.Buffered(1)),
            pl.BlockSpec((tm, dm), lambda m, n: (jnp.maximum(m - npt, 0), 0), pipeline_mode=pl.Buffered(1)),
            pl.BlockSpec((1, dm), lambda m, n: (0, 0)),
            pl.BlockSpec((dm, GROUP_WIDTH), lambda m, n: (0, (n % 3) * 3 + n // 3)),
            pl.BlockSpec((2, HEAD_DIM), lambda m, n: (0, 0)),
            pl.BlockSpec((tm, HEAD_DIM), lambda m, n: (m, 0)),
            pl.BlockSpec((tm, HEAD_DIM), lambda m, n: (m, 0)),
        ],
        out_specs=[pl.BlockSpec((tm, dm), lambda m, n: (m, 0))]
        + [out_spec(g, d) for g, (_, d) in enumerate(ATTN_GROUPS)],
        out_shape=[jax.ShapeDtypeStruct((m_tok, dm), BF)]
        + [jax.ShapeDtypeStruct((3, d, m_tok // d, GROUP_WIDTH), BF) for _, d in ATTN_GROUPS],
        scratch_shapes=[pltpu.VMEM((tm, dm), BF), pltpu.VMEM((HEADS_PER_GROUP, tm, HEAD_DIM), F32)],
        compiler_params=_params(2),
        name="inproj_attn",
    )(xp, xs, g1, w_in, qk_gain, cos_t, sin_t)
    return outs[0], outs[1:]


def _matmul_kernel(a_ref, w_ref, o_ref):
    o_ref[...] = jnp.dot(a_ref[...], w_ref[...].astype(BF),
                         preferred_element_type=F32).astype(o_ref.dtype)


def _inproj_rest(h, w_in, col0, tm=2048, tn=512):
    m_tok, dm = h.shape
    n_cols = w_in.shape[1] - col0
    tm = min(tm, m_tok)
    off = col0 // tn
    return pl.pallas_call(
        _matmul_kernel,
        grid=(m_tok // tm, n_cols // tn),
        in_specs=[pl.BlockSpec((tm, dm), lambda m, n: (m, 0)),
                  pl.BlockSpec((dm, tn), lambda m, n: (0, off + n))],
        out_specs=pl.BlockSpec((tm, tn), lambda m, n: (m, n)),
        out_shape=jax.ShapeDtypeStruct((m_tok, n_cols), BF),
        compiler_params=_params(2),
        name="inproj_rest",
    )(h, w_in)


def _attn_kernel(q_ref, kp_ref, kc_ref, kn_ref, vp_ref, vc_ref, vn_ref, o_ref, lse_ref, *, bounds, tq):
    sub = ATTN_SUB_BLOCK
    j0 = pl.program_id(1) * tq
    lo = jnp.int32(bounds[0])
    hi = jnp.int32(bounds[1])
    for b0, b1 in zip(bounds[1:-1], bounds[2:]):
        lo = jnp.where(j0 >= b0, b0, lo)
        hi = jnp.where(j0 >= b0, b1, hi)
    k_all = jnp.concatenate([kp_ref[...], kc_ref[...], kn_ref[...]], axis=0)
    v_all = jnp.concatenate([vp_ref[...], vc_ref[...], vn_ref[...]], axis=0)
    scale = 1.0 / math.sqrt(HEAD_DIM)
    lane = lax.broadcasted_iota(jnp.int32, (1, LANES), 1)
    n_keys = sub + 2 * BAND
    chains = [(u, h) for u in range(tq // sub) for h in range(HEADS_PER_GROUP)]

    def head(h):
        return slice(h * HEAD_DIM, (h + 1) * HEAD_DIM)

    def keys(u):
        return slice((u + 1) * sub - BAND, (u + 1) * sub - BAND + n_keys)

    valid = []
    for u in range(tq // sub):
        qpos = j0 + u * sub + lax.broadcasted_iota(jnp.int32, (sub, 1), 0)
        kpos = j0 + u * sub - BAND + lax.broadcasted_iota(jnp.int32, (1, n_keys), 1)
        valid.append((jnp.abs(kpos - qpos) <= BAND) & (kpos >= lo) & (kpos < hi))
    s = [lax.dot_general(q_ref[u * sub:(u + 1) * sub, head(h)], k_all[keys(u), head(h)],
                         (((1,), (1,)), ((), ())), preferred_element_type=F32) for u, h in chains]
    s = [jnp.where(valid[u], si * scale, NEG) for (u, h), si in zip(chains, s)]
    mx = [jnp.max(si, axis=-1, keepdims=True) for si in s]
    p = [jnp.exp(si - mi) for si, mi in zip(s, mx)]
    den = [jnp.sum(pi, axis=-1, keepdims=True) for pi in p]
    o = [jnp.dot(pi.astype(BF), v_all[keys(u), head(h)], preferred_element_type=F32)
         for (u, h), pi in zip(chains, p)]
    for u in range(tq // sub):
        rows = slice(u * sub, (u + 1) * sub)
        lse_all = jnp.zeros((sub, LANES), F32)
        for h in range(HEADS_PER_GROUP):
            c = chains.index((u, h))
            o_ref[rows, head(h)] = (o[c] / den[c]).astype(BF)
            lse_all = jnp.where(lane == h, mx[c] + jnp.log(den[c]), lse_all)
        lse_ref[rows, :] = lse_all


def _band_attention(qkv, d, seq_bounds):
    _, _, n_j, _ = qkv.shape
    sub = ATTN_SUB_BLOCK
    bounds = tuple(b // d for b in seq_bounds)
    tq = max(t for t in (sub, ATTN_Q_BLOCK) if all(b % t == 0 for b in bounds))
    n_blocks = n_j // tq
    n_sub = n_j // sub
    per = tq // sub

    def spec(which, shift):
        if shift == 0:
            return pl.BlockSpec((None, None, tq, GROUP_WIDTH), lambda r, i: (which, r, i, 0))
        halo = -1 if shift < 0 else per
        return pl.BlockSpec((None, None, sub, GROUP_WIDTH),
                            lambda r, i: (which, r, jnp.clip(i * per + halo, 0, n_sub - 1), 0))

    return pl.pallas_call(
        functools.partial(_attn_kernel, bounds=bounds, tq=tq),
        grid=(d, n_blocks),
        in_specs=[spec(0, 0), spec(1, -1), spec(1, 0), spec(1, 1), spec(2, -1), spec(2, 0), spec(2, 1)],
        out_specs=[pl.BlockSpec((None, tq, GROUP_WIDTH), lambda r, i: (r, i, 0)),
                   pl.BlockSpec((None, tq, LANES), lambda r, i: (r, i, 0))],
        out_shape=[jax.ShapeDtypeStruct((d, n_j, GROUP_WIDTH), BF),
                   jax.ShapeDtypeStruct((d, n_j, LANES), F32)],
        compiler_params=_params(2),
        name=f"band_attention_d{d}",
    )(qkv, qkv, qkv, qkv, qkv, qkv, qkv)


def _sgu_kernel(u_ref, v_ref, ng_ref, ws_ref, bs_ref, o_ref, *, tb):
    vn = _rms(_gelu_tanh(v_ref[...].astype(F32)), ng_ref[...]).astype(BF)
    for g in range(SGU_GROUPS):
        cols = slice(g * SGU_GROUP_WIDTH, (g + 1) * SGU_GROUP_WIDTH)
        w = ws_ref[g].astype(BF)
        bias = bs_ref[:, g:g + 1]
        for c in range(tb // SGU_CHUNK):
            rows = slice(c * SGU_CHUNK, (c + 1) * SGU_CHUNK)
            mixed = jnp.dot(w, vn[rows, cols], preferred_element_type=F32) + bias
            u = _gelu_tanh(u_ref[rows, cols].astype(F32))
            o_ref[rows, cols] = (u * mixed).astype(BF)


def _sgu(z_rest, norm_g, w_s, b_s_t, tb=512):
    m_tok = z_rest.shape[0]
    return pl.pallas_call(
        functools.partial(_sgu_kernel, tb=tb),
        grid=(m_tok // tb,),
        in_specs=[pl.BlockSpec((tb, SGU_WIDTH), lambda i: (i, 0)),
                  pl.BlockSpec((tb, SGU_WIDTH), lambda i: (i, 1)),
                  pl.BlockSpec((1, SGU_WIDTH), lambda i: (0, 0)),
                  pl.BlockSpec((SGU_GROUPS, SGU_CHUNK, SGU_CHUNK), lambda i: (0, 0, 0)),
                  pl.BlockSpec((SGU_CHUNK, SGU_GROUPS), lambda i: (0, 0))],
        out_specs=pl.BlockSpec((tb, SGU_WIDTH), lambda i: (i, 0)),
        out_shape=jax.ShapeDtypeStruct((m_tok, SGU_WIDTH), BF),
        compiler_params=_params(1),
        name="sgu",
    )(z_rest, z_rest, norm_g, w_s, b_s_t)


def _mem_kv_kernel(mem_ref, ng_ref, w_ref, kg_ref, o_ref, *, n_k_tiles, tn):
    n = pl.program_id(1)
    hm = _rms(mem_ref[...], ng_ref[...]).astype(BF)
    kv = jnp.dot(hm, w_ref[...].astype(BF), preferred_element_type=F32)

    @pl.when(n < n_k_tiles)
    def _():
        for h in range(tn // MEM_HEAD_DIM):
            cols = slice(h * MEM_HEAD_DIM, (h + 1) * MEM_HEAD_DIM)
            o_ref[:, cols] = _rms(kv[:, cols], kg_ref[...]).astype(BF)

    @pl.when(n >= n_k_tiles)
    def _():
        o_ref[...] = kv.astype(BF)


def _mem_kv(mem, norm_g, w_kv, k_gain, tn=512):
    n_b, n_mem, dm = mem.shape
    width = w_kv.shape[1]
    return pl.pallas_call(
        functools.partial(_mem_kv_kernel, n_k_tiles=MEM_WIDTH // tn, tn=tn),
        grid=(n_b, width // tn),
        in_specs=[pl.BlockSpec((None, n_mem, dm), lambda b, n: (b, 0, 0)),
                  pl.BlockSpec((1, dm), lambda b, n: (0, 0)),
                  pl.BlockSpec((dm, tn), lambda b, n: (0, n)),
                  pl.BlockSpec((1, MEM_HEAD_DIM), lambda b, n: (0, 0))],
        out_specs=pl.BlockSpec((None, n_mem, tn), lambda b, n: (b, 0, n)),
        out_shape=jax.ShapeDtypeStruct((n_b, n_mem, width), BF),
        compiler_params=_params(2),
        name="mem_kv",
    )(mem, norm_g, w_kv, k_gain)


def _mem_attn_kernel(q_ref, kv_ref, qg_ref, o_ref):
    scale = 1.0 / math.sqrt(MEM_HEAD_DIM)
    for h in range(MEM_HEADS):
        cols = slice(h * MEM_HEAD_DIM, (h + 1) * MEM_HEAD_DIM)
        q = _rms(q_ref[:, cols].astype(F32), qg_ref[...]).astype(BF)
        k = kv_ref[:, cols]
        v = kv_ref[:, MEM_WIDTH + h * MEM_HEAD_DIM:MEM_WIDTH + (h + 1) * MEM_HEAD_DIM]
        s = lax.dot_general(q, k, (((1,), (1,)), ((), ())), preferred_element_type=F32) * scale
        mx = jnp.max(s, axis=-1, keepdims=True)
        p = jnp.exp(s - mx)
        den = jnp.sum(p, axis=-1, keepdims=True)
        o = jnp.dot(p.astype(BF), v, preferred_element_type=F32)
        o_ref[:, cols] = (o / den).astype(BF)


def _mem_attn(z_rest, kv, q_gain, batch_bounds, tm=512):
    m_tok = z_rest.shape[0]
    n_mem = kv.shape[1]
    q_block = (2 * SGU_WIDTH) // MEM_WIDTH

    def batch_of(i):
        b = 0
        for bound in batch_bounds[1:-1]:
            b = b + (i * tm >= bound).astype(jnp.int32)
        return b

    return pl.pallas_call(
        _mem_attn_kernel,
        grid=(m_tok // tm,),
        in_specs=[pl.BlockSpec((tm, MEM_WIDTH), lambda i: (i, q_block)),
                  pl.BlockSpec((None, n_mem, 2 * MEM_WIDTH), lambda i: (batch_of(i), 0, 0)),
                  pl.BlockSpec((1, MEM_HEAD_DIM), lambda i: (0, 0))],
        out_specs=pl.BlockSpec((tm, MEM_WIDTH), lambda i: (i, 0)),
        out_shape=jax.ShapeDtypeStruct((m_tok, MEM_WIDTH), BF),
        compiler_params=_params(1),
        name="mem_attn",
    )(z_rest, kv, q_gain)


def _merge_kernel(oa0_ref, oa1_ref, oa2_ref, l0_ref, l1_ref, l2_ref, ob_ref, om_ref,
                  g0_ref, g1_ref, g2_ref, wa_ref, wb_ref, wm_ref, o_ref, oa_scr, lse_scr, *, tm):
    n_h = HEADS_PER_GROUP
    n_g = len(ATTN_GROUPS)
    for g, (oa_ref, l_ref, (_, d)) in enumerate(zip((oa0_ref, oa1_ref, oa2_ref),
                                                    (l0_ref, l1_ref, l2_ref), ATTN_GROUPS)):
        for r in range(d):
            rows = pl.ds(r, tm // d, stride=d) if d > 1 else slice(None)
            o = oa_ref[r].astype(F32)
            for h in range(n_h):
                oa_scr[g * n_h + h, rows, :] = o[:, h * HEAD_DIM:(h + 1) * HEAD_DIM]
            lse_scr[g, rows, :] = l_ref[r]
    lses = [lse_scr[g] for g in range(n_g)]
    mx = jnp.maximum(jnp.maximum(lses[0], lses[1]), lses[2])
    es = [jnp.exp(l - mx) for l in lses]
    den = es[0] + es[1] + es[2]
    slabs = []
    for g in range(n_g):
        alpha = es[g] / den
        slabs += [(oa_scr[g * n_h + h] * alpha[:, h:h + 1]).astype(BF) for h in range(n_h)]
    pa = jnp.dot(jnp.concatenate(slabs, axis=1), wa_ref[...], preferred_element_type=F32)
    merged = jax.nn.sigmoid(g0_ref[...].astype(F32)) * pa
    pb = jnp.dot(ob_ref[...], wb_ref[...], preferred_element_type=F32)
    merged = merged + jax.nn.sigmoid(g1_ref[...].astype(F32)) * pb
    pm = jnp.dot(om_ref[...], wm_ref[...], preferred_element_type=F32)
    merged = merged + jax.nn.sigmoid(g2_ref[...].astype(F32)) * pm
    o_ref[...] = merged.astype(BF)


def _resident(shape):
    return pl.BlockSpec(shape, lambda i: (0,) * len(shape), pipeline_mode=pl.Buffered(1))


def _merge(oas, lses, ob, om, z_rest, wa, wb, wm, dm, tm=512):
    m_tok = ob.shape[0]
    gate0 = (2 * SGU_WIDTH + MEM_WIDTH) // dm
    oa_specs = [pl.BlockSpec((d, tm // d, GROUP_WIDTH), lambda i: (0, i, 0)) for _, d in ATTN_GROUPS]
    l_specs = [pl.BlockSpec((d, tm // d, LANES), lambda i: (0, i, 0)) for _, d in ATTN_GROUPS]
    g_specs = [pl.BlockSpec((tm, dm), lambda i, b=b: (i, gate0 + b)) for b in range(3)]
    return pl.pallas_call(
        functools.partial(_merge_kernel, tm=tm),
        grid=(m_tok // tm,),
        in_specs=oa_specs + l_specs
        + [pl.BlockSpec((tm, SGU_WIDTH), lambda i: (i, 0)), pl.BlockSpec((tm, MEM_WIDTH), lambda i: (i, 0))]
        + g_specs + [_resident(wa.shape), _resident(wb.shape), _resident(wm.shape)],
        out_specs=pl.BlockSpec((tm, dm), lambda i: (i, 0)),
        out_shape=jax.ShapeDtypeStruct((m_tok, dm), BF),
        scratch_shapes=[pltpu.VMEM((len(ATTN_GROUPS) * HEADS_PER_GROUP, tm, HEAD_DIM), F32),
                        pltpu.VMEM((len(ATTN_GROUPS), tm, LANES), F32)],
        compiler_params=_params(1),
        name="merge",
    )(*oas, *lses, ob, om, z_rest, z_rest, z_rest, wa, wb, wm)


def _out_kernel(mg_ref, xp_in, xs_in, wo_ref, g2_ref, wr_ref, xp_ref, xs_ref, h2_ref, lg_ref, *,
                n_prompt_tiles):
    i = pl.program_id(0)
    proj = jnp.dot(mg_ref[...], wo_ref[...], preferred_element_type=F32)

    @pl.when(i < n_prompt_tiles)
    def _():
        xp_ref[...] = xp_in[...] + proj

    @pl.when(i >= n_prompt_tiles)
    def _():
        xs_ref[...] = xs_in[...] + proj

    x2 = jnp.where(i < n_prompt_tiles, xp_in[...], xs_in[...]) + proj
    h2 = _rms(x2, g2_ref[...])
    h2_ref[...] = h2
    h_hi = h2.astype(BF)
    h_lo = (h2 - h_hi.astype(F32)).astype(BF)
    wr = wr_ref[...]
    w_hi = wr.astype(BF)
    w_lo = (wr - w_hi.astype(F32)).astype(BF)
    nt = (((1,), (1,)), ((), ()))
    lg = lax.dot_general(w_hi, h_hi, nt, preferred_element_type=F32)
    lg = lg + lax.dot_general(w_hi, h_lo, nt, preferred_element_type=F32)
    lg = lg + lax.dot_general(w_lo, h_hi, nt, preferred_element_type=F32)
    lg_ref[...] = lg


def _out_proj(merged, xp, xs, wo, g2, w_router_t, tm=256):
    n_prompt, dm = xp.shape
    m_tok = n_prompt + xs.shape[0]
    n_e = w_router_t.shape[0]
    npt = n_prompt // tm
    return pl.pallas_call(
        functools.partial(_out_kernel, n_prompt_tiles=npt),
        grid=(m_tok // tm,),
        in_specs=[pl.BlockSpec((tm, dm), lambda i: (i, 0)),
                  pl.BlockSpec((tm, dm), lambda i: (jnp.minimum(i, npt - 1), 0)),
                  pl.BlockSpec((tm, dm), lambda i: (jnp.maximum(i - npt, 0), 0)),
                  _resident(wo.shape),
                  pl.BlockSpec((1, dm), lambda i: (0, 0)),
                  pl.BlockSpec((n_e, dm), lambda i: (0, 0))],
        out_specs=[pl.BlockSpec((tm, dm), lambda i: (jnp.minimum(i, npt - 1), 0)),
                   pl.BlockSpec((tm, dm), lambda i: (jnp.maximum(i - npt, 0), 0)),
                   pl.BlockSpec((tm, dm), lambda i: (i, 0)),
                   pl.BlockSpec((n_e, tm), lambda i: (0, i))],
        out_shape=[jax.ShapeDtypeStruct((n_prompt, dm), F32),
                   jax.ShapeDtypeStruct((m_tok - n_prompt, dm), F32),
                   jax.ShapeDtypeStruct((m_tok, dm), F32),
                   jax.ShapeDtypeStruct((n_e, m_tok), F32)],
        compiler_params=_params(1),
        name="out_proj",
    )(merged, xp, xs, wo, g2, w_router_t)


def _router_kernel(lg_ref, u_ref, bl_ref, ui_ref, idx_ref, gate_ref, *, cap):
    n_e, n_r, _ = lg_ref.shape
    lg = lg_ref[...]
    ex = jnp.exp(lg - jnp.max(lg, axis=0, keepdims=True))
    aff = ex / jnp.sum(ex, axis=0, keepdims=True)
    key = lax.bitcast_convert_type(aff, jnp.int32)

    def count(mask):
        return jnp.sum(jnp.sum(mask.astype(F32), axis=2, keepdims=True), axis=1, keepdims=True)

    thr = jnp.zeros((n_e, 1, 1), jnp.int32)
    for bit in range(30, -1, -1):
        cand = thr | jnp.int32(1 << bit)
        thr = jnp.where(count(key >= cand) >= cap, cand, thr)
    gt = key > thr
    eq = key == thr
    need = cap - count(gt)

    def prefix(mask_f32):
        m2 = mask_f32.reshape(n_e * n_r, LANES)
        incl = jnp.dot(m2.astype(BF), u_ref[...], preferred_element_type=F32)
        tot = jnp.broadcast_to(incl[:, LANES - 1:LANES], incl.shape)
        rowoff = jnp.dot(bl_ref[...], tot.astype(BF), preferred_element_type=F32)
        return rowoff + incl - m2, incl

    eq_rank, _ = prefix(eq.astype(F32))
    take_eq = eq & (eq_rank.reshape(n_e, n_r, LANES) < need)
    sel = jnp.where(gt, 1.0, jnp.where(take_eq, 1.0, 0.0))
    _, incl = prefix(sel)
    sel2 = sel.reshape(n_e * n_r, LANES).astype(BF)

    ones8 = jnp.ones((8, LANES), BF)
    slot = lax.broadcasted_iota(jnp.int32, (cap, 1), 0).astype(F32)
    row_id = lax.broadcasted_iota(jnp.int32, (cap, n_r), 1).astype(F32)
    lane_id = lax.broadcasted_iota(jnp.int32, (cap, LANES), 1).astype(F32)
    nt = (((1,), (1,)), ((), ()))
    for e in range(n_e):
        rows = slice(e * n_r, (e + 1) * n_r)
        tot_row = lax.dot_general(ones8, sel2[rows], nt, preferred_element_type=F32)[0:1]
        cum_row = jnp.dot(tot_row.astype(BF), ui_ref[...], preferred_element_type=F32)
        r_s = jnp.sum((cum_row <= slot).astype(F32), axis=1, keepdims=True)
        onehot = row_id == r_s
        oh = jnp.where(onehot, 1.0, 0.0).astype(BF)
        row_start = jnp.sum(jnp.where(onehot, cum_row - tot_row, 0.0), axis=1, keepdims=True)
        g_incl = jnp.dot(oh, incl[rows].astype(BF), preferred_element_type=F32)
        l_s = jnp.sum((g_incl <= slot - row_start).astype(F32), axis=1, keepdims=True)
        idx_ref[e] = (r_s * LANES + l_s).astype(jnp.int32)
        a = aff[e]
        a1 = a.astype(BF)
        a2 = (a - a1.astype(F32)).astype(BF)
        a3 = (a - a1.astype(F32) - a2.astype(F32)).astype(BF)
        g_aff = (jnp.dot(oh, a1, preferred_element_type=F32)
                 + jnp.dot(oh, a2, preferred_element_type=F32)) + jnp.dot(oh, a3, preferred_element_type=F32)
        gate_ref[e] = jnp.sum(jnp.where(lane_id == l_s, g_aff, 0.0), axis=1, keepdims=True)


def _route(logits, cap):
    n_g, n_e, n_r, _ = logits.shape
    a = np.arange(LANES)
    upper = jnp.asarray(a[:, None] <= a[None, :], BF)
    i = np.arange(n_e * n_r)
    block_lower = jnp.asarray((i[:, None] // n_r == i[None, :] // n_r) & (i[None, :] < i[:, None]), BF)
    r = np.arange(n_r)
    upper_r = jnp.asarray(r[:, None] <= r[None, :], BF)
    return pl.pallas_call(
        functools.partial(_router_kernel, cap=cap),
        grid=(n_g,),
        in_specs=[pl.BlockSpec((None, n_e, n_r, LANES), lambda g: (g, 0, 0, 0)),
                  pl.BlockSpec((LANES, LANES), lambda g: (0, 0)),
                  pl.BlockSpec((n_e * n_r, n_e * n_r), lambda g: (0, 0)),
                  pl.BlockSpec((n_r, n_r), lambda g: (0, 0))],
        out_specs=[pl.BlockSpec((None, n_e, cap, 1), lambda g: (g, 0, 0, 0)),
                   pl.BlockSpec((None, n_e, cap, 1), lambda g: (g, 0, 0, 0))],
        out_shape=[jax.ShapeDtypeStruct((n_g, n_e, cap, 1), jnp.int32),
                   jax.ShapeDtypeStruct((n_g, n_e, cap, 1), F32)],
        compiler_params=_params(1),
        name="router",
    )(logits, upper, block_lower, upper_r)


def _expert_kernel(idx_ref, h2_hbm, gate_ref, wg_ref, wu_ref, wd_ref, xp_in, xs_in,
                   yp_hbm, ys_hbm, xb, stage, acc, sem_in, sem_out, *, cap, n_group_tok, chunk):
    del xp_in, xs_in
    e = pl.program_id(0)
    f = pl.program_id(1)
    n_f = pl.num_programs(1)
    n_slots = 2 * cap
    n_chunks = n_slots // chunk

    def row_copies(hbm, base, tok_off, to_vmem, sem):
        def body(i, carry):
            s0 = pl.multiple_of(i * ROW_DMA_UNROLL, ROW_DMA_UNROLL)
            for j in range(ROW_DMA_UNROLL):
                t = idx_ref[base + s0 + j] - tok_off
                src, dst = hbm.at[pl.ds(t, 1)], stage.at[pl.ds(s0 + j, 1)]
                if not to_vmem:
                    src, dst = dst, src
                pltpu.make_async_copy(src, dst, sem).start()
            return carry
        lax.fori_loop(0, chunk // ROW_DMA_UNROLL, body, 0)

    def wait_rows(hbm, to_vmem, sem):
        src, dst = hbm.at[pl.ds(0, chunk)], stage
        if not to_vmem:
            src, dst = dst, src
        pltpu.make_async_copy(src, dst, sem).wait()

    @pl.when(f == 0)
    def _():
        for c in range(n_chunks):
            row_copies(h2_hbm, e * n_slots + c * chunk, 0, True, sem_in)
            wait_rows(h2_hbm, True, sem_in)
            xb[c * chunk:(c + 1) * chunk, :] = stage[...].astype(BF)
        acc[...] = jnp.zeros_like(acc)

    x = xb[...]
    a = jnp.dot(x, wg_ref[...].astype(BF), preferred_element_type=F32)
    b = jnp.dot(x, wu_ref[...].astype(BF), preferred_element_type=F32)
    hmid = (a * jax.nn.sigmoid(a) * b).astype(BF)
    dm = acc.shape[1]
    for n in range(dm // DOWN_TILE):
        cols = slice(n * DOWN_TILE, (n + 1) * DOWN_TILE)
        acc[:, cols] += jnp.dot(hmid, wd_ref[:, cols].astype(BF), preferred_element_type=F32)

    @pl.when(f == n_f - 1)
    def _():
        for c in range(n_chunks):
            group = (c * chunk) // cap
            y_hbm = (yp_hbm, ys_hbm)[group]
            base = e * n_slots + c * chunk
            tok_off = group * n_group_tok
            rows = slice(c * chunk, (c + 1) * chunk)
            row_copies(y_hbm, base, tok_off, True, sem_in)
            wait_rows(y_hbm, True, sem_in)
            stage[...] = stage[...] + acc[rows, :] * gate_ref[rows, :]
            row_copies(y_hbm, base, tok_off, False, sem_out)
            wait_rows(y_hbm, False, sem_out)


def _experts(idx_flat, h2, gate, w_gate, w_up, w_down, x2p, x2s, cap, tf=256):
    n_e, dm, ff = w_gate.shape
    n_slots = 2 * cap
    chunk = min(GATHER_CHUNK, cap)
    n_group_tok = x2p.shape[0]
    any_spec = pl.BlockSpec(memory_space=pl.ANY)
    grid_spec = pltpu.PrefetchScalarGridSpec(
        num_scalar_prefetch=1,
        grid=(n_e, ff // tf),
        in_specs=[any_spec,
                  pl.BlockSpec((None, n_slots, 1), lambda e, f, idx: (e, 0, 0)),
                  pl.BlockSpec((None, dm, tf), lambda e, f, idx: (e, 0, f)),
                  pl.BlockSpec((None, dm, tf), lambda e, f, idx: (e, 0, f)),
                  pl.BlockSpec((None, tf, dm), lambda e, f, idx: (e, f, 0)),
                  any_spec, any_spec],
        out_specs=[any_spec, any_spec],
        scratch_shapes=[pltpu.VMEM((n_slots, dm), BF),
                        pltpu.VMEM((chunk, dm), F32),
                        pltpu.VMEM((n_slots, dm), F32),
                        pltpu.SemaphoreType.DMA(()),
                        pltpu.SemaphoreType.DMA(())],
    )
    return pl.pallas_call(
        functools.partial(_expert_kernel, cap=cap, n_group_tok=n_group_tok, chunk=chunk),
        grid_spec=grid_spec,
        out_shape=[jax.ShapeDtypeStruct(x2p.shape, F32), jax.ShapeDtypeStruct(x2s.shape, F32)],
        input_output_aliases={6: 0, 7: 1},
        compiler_params=_params(2),
        name="experts",
    )(idx_flat, h2, gate, w_gate, w_up, w_down, x2p, x2s)


def _rope_tables(seq_lens):
    half = ROPE_DIM // 2
    inv = ROPE_THETA ** (-jnp.arange(0, ROPE_DIM, 2, dtype=F32) / ROPE_DIM)
    pos = jnp.concatenate([jnp.arange(s, dtype=F32) for s in seq_lens])
    ang = pos[:, None] * inv[None, :]
    cos, sin = jnp.cos(ang), jnp.sin(ang)
    n = pos.shape[0]
    cos_t = jnp.concatenate([cos, cos, jnp.ones((n, HEAD_DIM - 2 * half), F32)], axis=1)
    sin_t = jnp.concatenate([-sin, sin, jnp.zeros((n, HEAD_DIM - 2 * half), F32)], axis=1)
    return cos_t, sin_t


def kernel(x_prompt, x_sample, mem_prompt, mem_sample, norm1_g, w_in, q_norm_a, k_norm_a, sgu_norm_g,
           sgu_w, sgu_b, mem_norm_g, w_mem_kv, q_norm_m, k_norm_m, w_proj_a, w_proj_b, w_proj_m, w_out,
           norm2_g, w_router, w_gate, w_up, w_down):
    assert norm1_g.shape[0] == 1, "one layer"
    bp, sp, dm = x_prompt.shape
    bs, ss, _ = x_sample.shape
    n_prompt, n_sample = bp * sp, bs * ss
    assert n_prompt == n_sample, "both request groups route the same number of tokens"
    n_e = w_router.shape[-1]
    cap = CAPACITY_FACTOR * n_prompt // n_e
    seq_lens = (sp,) * bp + (ss,) * bs
    seq_bounds = tuple(int(v) for v in np.concatenate([[0], np.cumsum(seq_lens)]))

    xp = x_prompt.reshape(n_prompt, dm)
    xs = x_sample.reshape(n_sample, dm)
    mem = jnp.concatenate([mem_prompt, mem_sample], axis=0)
    cos_t, sin_t = _rope_tables(seq_lens)
    qk_gain = jnp.concatenate([q_norm_a, k_norm_a], axis=0)

    def layer0(w):
        return w.reshape(w.shape[1:])

    w_in, sgu_w, sgu_b, w_mem_kv, w_proj_a, w_proj_b, w_proj_m, w_out, w_router, w_gate, w_up, w_down = map(
        layer0, (w_in, sgu_w, sgu_b, w_mem_kv, w_proj_a, w_proj_b, w_proj_m, w_out, w_router, w_gate, w_up,
                 w_down))

    h, qkvs = _inproj_attn(xp, xs, norm1_g, w_in, qk_gain, cos_t, sin_t)
    z_rest = _inproj_rest(h, w_in, 3 * ATTN_WIDTH)

    oas, lses = [], []
    for qkv, (_, d) in zip(qkvs, ATTN_GROUPS):
        o, lse = _band_attention(qkv, d, seq_bounds)
        oas.append(o)
        lses.append(lse)
    ob = _sgu(z_rest, sgu_norm_g, sgu_w, sgu_b.T)
    kv = _mem_kv(mem, mem_norm_g, w_mem_kv, k_norm_m)
    om = _mem_attn(z_rest, kv, q_norm_m, seq_bounds)

    merged = _merge(oas, lses, ob, om, z_rest, _cast_bf16(w_proj_a), _cast_bf16(w_proj_b),
                    _cast_bf16(w_proj_m), dm)
    x2p, x2s, h2, logits_t = _out_proj(merged, xp, xs, _cast_bf16(w_out), norm2_g, w_router.T)

    logits = logits_t.reshape(n_e, 2, n_prompt // LANES, LANES).transpose(1, 0, 2, 3)
    idx, gate = _route(logits, cap)
    tok_off = jnp.arange(2, dtype=jnp.int32)[:, None, None] * n_prompt
    idx_flat = (idx[..., 0] + tok_off).transpose(1, 0, 2).reshape(-1)
    gate = gate.transpose(1, 0, 2, 3).reshape(n_e, 2 * cap, 1)

    yp, ys = _experts(idx_flat, h2, gate, w_gate, w_up, w_down, x2p, x2s, cap)
    return yp.reshape(bp, sp, dm), ys.reshape(bs, ss, dm)
```

```python
import functools
import math

import numpy as np
import jax
import jax.numpy as jnp
from jax import lax
from jax.experimental import pallas as pl
from jax.experimental.pallas import tpu as pltpu

BF = jnp.bfloat16
F32 = jnp.float32

EPS = 1e-6
NEG = -1e30
LANES = 128
HEAD_DIM = 128
ATTN_GROUPS = ((128, 1), (512, 4), (2048, 16))
HEADS_PER_GROUP = 4
GROUP_WIDTH = HEADS_PER_GROUP * HEAD_DIM
ATTN_WIDTH = len(ATTN_GROUPS) * GROUP_WIDTH
BAND = 64
ROPE_THETA = 500000.0
ROPE_DIM = HEAD_DIM // 4
SGU_CHUNK = 128
SGU_GROUPS = 4
SGU_WIDTH = 1536
SGU_GROUP_WIDTH = SGU_WIDTH // SGU_GROUPS
MEM_HEADS = 4
MEM_HEAD_DIM = 256
MEM_WIDTH = MEM_HEADS * MEM_HEAD_DIM
CAPACITY_FACTOR = 2
ATTN_SUB_BLOCK = 128
ATTN_Q_BLOCK = 256
INPROJ_ROW_SPLIT = 4
GATHER_CHUNK = 512
ROW_DMA_UNROLL = 8
DOWN_TILE = 256
VMEM_LIMIT = 52 * 1024 * 1024


def _params(n_axes, vmem=VMEM_LIMIT):
    return pltpu.CompilerParams(dimension_semantics=("arbitrary",) * n_axes, vmem_limit_bytes=vmem)


def _rms(x, gain):
    ms = jnp.mean(x * x, axis=-1, keepdims=True)
    return x * lax.rsqrt(ms + EPS) * gain


def _gelu_tanh(x):
    c = math.sqrt(2.0 / math.pi)
    return x * (0.5 * (1.0 + jnp.tanh(c * (x + 0.044715 * (x * x * x)))))


def _cast_kernel(x_ref, o_ref):
    o_ref[...] = x_ref[...].astype(BF)


def _cast_bf16(w, rows=512):
    r, c = w.shape
    rows = min(rows, r)
    return pl.pallas_call(
        _cast_kernel,
        grid=(r // rows,),
        in_specs=[pl.BlockSpec((rows, c), lambda i: (i, 0))],
        out_specs=pl.BlockSpec((rows, c), lambda i: (i, 0)),
        out_shape=jax.ShapeDtypeStruct((r, c), BF),
        compiler_params=_params(1),
        name="cast_bf16",
    )(w)


def _qk_norm_rope(acc, gain, cos_t, sin_t):
    lane = lax.broadcasted_iota(jnp.int32, (1, HEAD_DIM), 1)
    half = ROPE_DIM // 2
    outs = []
    for h in range(HEADS_PER_GROUP):
        y = _rms(acc[:, h * HEAD_DIM:(h + 1) * HEAD_DIM], gain)
        partner = jnp.where(lane < half, pltpu.roll(y, HEAD_DIM - half, 1), pltpu.roll(y, half, 1))
        outs.append(y * cos_t + partner * sin_t)
    return jnp.concatenate(outs, axis=1)


def _store_dilated(val, scr, o_ref, d, row0):
    n_rows = val.shape[0]
    if d == 1:
        o_ref[0, row0:row0 + n_rows, :] = val.astype(BF)
        return
    for h in range(HEADS_PER_GROUP):
        scr[h, row0:row0 + n_rows, :] = val[:, h * HEAD_DIM:(h + 1) * HEAD_DIM]
    for r in range(d):
        o_ref[r, row0 // d:(row0 + n_rows) // d, :] = jnp.concatenate(
            [scr[h, pl.ds(row0 + r, n_rows // d, stride=d), :] for h in range(HEADS_PER_GROUP)],
            axis=1).astype(BF)


def _norm1_kernel(xp_ref, xs_ref, g_ref, h_ref, *, n_prompt_tiles):
    i = pl.program_id(0)
    for x_ref, is_mine in ((xp_ref, i < n_prompt_tiles), (xs_ref, i >= n_prompt_tiles)):
        @pl.when(is_mine)
        def _(x_ref=x_ref):
            h_ref[...] = _rms(x_ref[...], g_ref[...]).astype(BF)


def _norm1(xp, xs, g1, tm=512):
    dm = xp.shape[1]
    m_tok = xp.shape[0] + xs.shape[0]
    npt = xp.shape[0] // tm
    return pl.pallas_call(
        functools.partial(_norm1_kernel, n_prompt_tiles=npt),
        grid=(m_tok // tm,),
        in_specs=[pl.BlockSpec((tm, dm), lambda i: (jnp.minimum(i, npt - 1), 0)),
                  pl.BlockSpec((tm, dm), lambda i: (jnp.maximum(i - npt, 0), 0)),
                  pl.BlockSpec((1, dm), lambda i: (0, 0))],
        out_specs=pl.BlockSpec((tm, dm), lambda i: (i, 0)),
        out_shape=jax.ShapeDtypeStruct((m_tok, dm), BF),
        compiler_params=_params(1),
        name="norm1",
    )(xp, xs, g1)


def _inproj_attn_kernel(h_ref, w_ref, qkg_ref, cos_ref, sin_ref, o0_ref, o1_ref, o2_ref, scr, *, tm):
    n = pl.program_id(1)
    group = n // 3
    which = n % 3
    w = w_ref[...].astype(BF)
    rq = tm // INPROJ_ROW_SPLIT

    def run(o_ref, d, rope):
        gain = qkg_ref[pl.ds(jnp.minimum(which, 1), 1), :]
        for b in range(INPROJ_ROW_SPLIT):
            rows = slice(b * rq, (b + 1) * rq)
            val = jnp.dot(h_ref[rows, :], w, preferred_element_type=F32)
            if rope:
                val = _qk_norm_rope(val, gain, cos_ref[rows, :], sin_ref[rows, :])
            _store_dilated(val, scr, o_ref, d, b * rq)

    for g, (o_ref, (_, d)) in enumerate(zip((o0_ref, o1_ref, o2_ref), ATTN_GROUPS)):
        @pl.when((group == g) & (which < 2))
        def _(o_ref=o_ref, d=d):
            run(o_ref, d, True)

        @pl.when((group == g) & (which == 2))
        def _(o_ref=o_ref, d=d):
            run(o_ref, d, False)


def _inproj_attn(h, w_in, qk_gain, cos_t, sin_t, tm=1024):
    m_tok, dm = h.shape
    n_steps = 3 * len(ATTN_GROUPS)

    def out_spec(g, d):
        return pl.BlockSpec((None, d, tm // d, GROUP_WIDTH),
                            lambda m, n: (jnp.clip(n - 3 * g, 0, 2), 0, m, 0))

    return pl.pallas_call(
        functools.partial(_inproj_attn_kernel, tm=tm),
        grid=(m_tok // tm, n_steps),
        in_specs=[
            pl.BlockSpec((tm, dm), lambda m, n: (m, 0)),
            pl.BlockSpec((dm, GROUP_WIDTH), lambda m, n: (0, (n % 3) * 3 + n // 3)),
            pl.BlockSpec((2, HEAD_DIM), lambda m, n: (0, 0)),
            pl.BlockSpec((tm, HEAD_DIM), lambda m, n: (m, 0)),
            pl.BlockSpec((tm, HEAD_DIM), lambda m, n: (m, 0)),
        ],
        out_specs=[out_spec(g, d) for g, (_, d) in enumerate(ATTN_GROUPS)],
        out_shape=[jax.ShapeDtypeStruct((3, d, m_tok // d, GROUP_WIDTH), BF) for _, d in ATTN_GROUPS],
        scratch_shapes=[pltpu.VMEM((HEADS_PER_GROUP, tm, HEAD_DIM), F32)],
        compiler_params=_params(2),
        name="inproj_attn",
    )(h, w_in, qk_gain, cos_t, sin_t)


def _matmul_kernel(a_ref, w_ref, o_ref):
    o_ref[...] = jnp.dot(a_ref[...], w_ref[...].astype(BF),
                         preferred_element_type=F32).astype(o_ref.dtype)


def _inproj_rest(h, w_in, col0, tm=2048, tn=512):
    m_tok, dm = h.shape
    n_cols = w_in.shape[1] - col0
    tm = min(tm, m_tok)
    off = col0 // tn
    return pl.pallas_call(
        _matmul_kernel,
        grid=(m_tok // tm, n_cols // tn),
        in_specs=[pl.BlockSpec((tm, dm), lambda m, n: (m, 0)),
                  pl.BlockSpec((dm, tn), lambda m, n: (0, off + n))],
        out_specs=pl.BlockSpec((tm, tn), lambda m, n: (m, n)),
        out_shape=jax.ShapeDtypeStruct((m_tok, n_cols), BF),
        compiler_params=_params(2),
        name="inproj_rest",
    )(h, w_in)


def _attn_kernel(q_ref, kp_ref, kc_ref, kn_ref, vp_ref, vc_ref, vn_ref, o_ref, lse_ref, *, bounds, tq):
    sub = ATTN_SUB_BLOCK
    j0 = pl.program_id(1) * tq
    lo = jnp.int32(bounds[0])
    hi = jnp.int32(bounds[1])
    for b0, b1 in zip(bounds[1:-1], bounds[2:]):
        lo = jnp.where(j0 >= b0, b0, lo)
        hi = jnp.where(j0 >= b0, b1, hi)
    k_all = jnp.concatenate([kp_ref[...], kc_ref[...], kn_ref[...]], axis=0)
    v_all = jnp.concatenate([vp_ref[...], vc_ref[...], vn_ref[...]], axis=0)
    scale = 1.0 / math.sqrt(HEAD_DIM)
    lane = lax.broadcasted_iota(jnp.int32, (1, LANES), 1)
    n_keys = sub + 2 * BAND
    chains = [(u, h) for u in range(tq // sub) for h in range(HEADS_PER_GROUP)]

    def head(h):
        return slice(h * HEAD_DIM, (h + 1) * HEAD_DIM)

    def keys(u):
        return slice((u + 1) * sub - BAND, (u + 1) * sub - BAND + n_keys)

    valid = []
    for u in range(tq // sub):
        qpos = j0 + u * sub + lax.broadcasted_iota(jnp.int32, (sub, 1), 0)
        kpos = j0 + u * sub - BAND + lax.broadcasted_iota(jnp.int32, (1, n_keys), 1)
        valid.append((jnp.abs(kpos - qpos) <= BAND) & (kpos >= lo) & (kpos < hi))
    s = [lax.dot_general(q_ref[u * sub:(u + 1) * sub, head(h)], k_all[keys(u), head(h)],
                         (((1,), (1,)), ((), ())), preferred_element_type=F32) for u, h in chains]
    s = [jnp.where(valid[u], si * scale, NEG) for (u, h), si in zip(chains, s)]
    mx = [jnp.max(si, axis=-1, keepdims=True) for si in s]
    p = [jnp.exp(si - mi) for si, mi in zip(s, mx)]
    den = [jnp.sum(pi, axis=-1, keepdims=True) for pi in p]
    o = [jnp.dot(pi.astype(BF), v_all[keys(u), head(h)], preferred_element_type=F32)
         for (u, h), pi in zip(chains, p)]
    for u in range(tq // sub):
        rows = slice(u * sub, (u + 1) * sub)
        lse_all = jnp.zeros((sub, LANES), F32)
        for h in range(HEADS_PER_GROUP):
            c = chains.index((u, h))
            o_ref[rows, head(h)] = (o[c] / den[c]).astype(BF)
            lse_all = jnp.where(lane == h, mx[c] + jnp.log(den[c]), lse_all)
        lse_ref[rows, :] = lse_all


def _band_attention(qkv, d, seq_bounds):
    _, _, n_j, _ = qkv.shape
    sub = ATTN_SUB_BLOCK
    bounds = tuple(b // d for b in seq_bounds)
    tq = max(t for t in (sub, ATTN_Q_BLOCK) if all(b % t == 0 for b in bounds))
    n_blocks = n_j // tq
    n_sub = n_j // sub
    per = tq // sub

    def spec(which, shift):
        if shift == 0:
            return pl.BlockSpec((None, None, tq, GROUP_WIDTH), lambda r, i: (which, r, i, 0))
        halo = -1 if shift < 0 else per
        return pl.BlockSpec((None, None, sub, GROUP_WIDTH),
                            lambda r, i: (which, r, jnp.clip(i * per + halo, 0, n_sub - 1), 0))

    return pl.pallas_call(
        functools.partial(_attn_kernel, bounds=bounds, tq=tq),
        grid=(d, n_blocks),
        in_specs=[spec(0, 0), spec(1, -1), spec(1, 0), spec(1, 1), spec(2, -1), spec(2, 0), spec(2, 1)],
        out_specs=[pl.BlockSpec((None, tq, GROUP_WIDTH), lambda r, i: (r, i, 0)),
                   pl.BlockSpec((None, tq, LANES), lambda r, i: (r, i, 0))],
        out_shape=[jax.ShapeDtypeStruct((d, n_j, GROUP_WIDTH), BF),
                   jax.ShapeDtypeStruct((d, n_j, LANES), F32)],
        compiler_params=_params(2),
        name=f"band_attention_d{d}",
    )(qkv, qkv, qkv, qkv, qkv, qkv, qkv)


def _sgu_kernel(u_ref, v_ref, ng_ref, ws_ref, bs_ref, o_ref, *, tb):
    vn = _rms(_gelu_tanh(v_ref[...].astype(F32)), ng_ref[...]).astype(BF)
    for g in range(SGU_GROUPS):
        cols = slice(g * SGU_GROUP_WIDTH, (g + 1) * SGU_GROUP_WIDTH)
        w = ws_ref[g].astype(BF)
        bias = bs_ref[:, g:g + 1]
        for c in range(tb // SGU_CHUNK):
            rows = slice(c * SGU_CHUNK, (c + 1) * SGU_CHUNK)
            mixed = jnp.dot(w, vn[rows, cols], preferred_element_type=F32) + bias
            u = _gelu_tanh(u_ref[rows, cols].astype(F32))
            o_ref[rows, cols] = (u * mixed).astype(BF)


def _sgu(z_rest, norm_g, w_s, b_s_t, tb=512):
    m_tok = z_rest.shape[0]
    return pl.pallas_call(
        functools.partial(_sgu_kernel, tb=tb),
        grid=(m_tok // tb,),
        in_specs=[pl.BlockSpec((tb, SGU_WIDTH), lambda i: (i, 0)),
                  pl.BlockSpec((tb, SGU_WIDTH), lambda i: (i, 1)),
                  pl.BlockSpec((1, SGU_WIDTH), lambda i: (0, 0)),
                  pl.BlockSpec((SGU_GROUPS, SGU_CHUNK, SGU_CHUNK), lambda i: (0, 0, 0)),
                  pl.BlockSpec((SGU_CHUNK, SGU_GROUPS), lambda i: (0, 0))],
        out_specs=pl.BlockSpec((tb, SGU_WIDTH), lambda i: (i, 0)),
        out_shape=jax.ShapeDtypeStruct((m_tok, SGU_WIDTH), BF),
        compiler_params=_params(1),
        name="sgu",
    )(z_rest, z_rest, norm_g, w_s, b_s_t)


def _mem_kv_kernel(mem_ref, ng_ref, w_ref, kg_ref, o_ref, *, n_k_tiles, tn):
    n = pl.program_id(1)
    hm = _rms(mem_ref[...], ng_ref[...]).astype(BF)
    kv = jnp.dot(hm, w_ref[...].astype(BF), preferred_element_type=F32)

    @pl.when(n < n_k_tiles)
    def _():
        for h in range(tn // MEM_HEAD_DIM):
            cols = slice(h * MEM_HEAD_DIM, (h + 1) * MEM_HEAD_DIM)
            o_ref[:, cols] = _rms(kv[:, cols], kg_ref[...]).astype(BF)

    @pl.when(n >= n_k_tiles)
    def _():
        o_ref[...] = kv.astype(BF)


def _mem_kv(mem, norm_g, w_kv, k_gain, tn=512):
    n_b, n_mem, dm = mem.shape
    width = w_kv.shape[1]
    return pl.pallas_call(
        functools.partial(_mem_kv_kernel, n_k_tiles=MEM_WIDTH // tn, tn=tn),
        grid=(n_b, width // tn),
        in_specs=[pl.BlockSpec((None, n_mem, dm), lambda b, n: (b, 0, 0)),
                  pl.BlockSpec((1, dm), lambda b, n: (0, 0)),
                  pl.BlockSpec((dm, tn), lambda b, n: (0, n)),
                  pl.BlockSpec((1, MEM_HEAD_DIM), lambda b, n: (0, 0))],
        out_specs=pl.BlockSpec((None, n_mem, tn), lambda b, n: (b, 0, n)),
        out_shape=jax.ShapeDtypeStruct((n_b, n_mem, width), BF),
        compiler_params=_params(2),
        name="mem_kv",
    )(mem, norm_g, w_kv, k_gain)


def _mem_attn_kernel(q_ref, kv_ref, qg_ref, o_ref):
    scale = 1.0 / math.sqrt(MEM_HEAD_DIM)
    for h in range(MEM_HEADS):
        cols = slice(h * MEM_HEAD_DIM, (h + 1) * MEM_HEAD_DIM)
        q = _rms(q_ref[:, cols].astype(F32), qg_ref[...]).astype(BF)
        k = kv_ref[:, cols]
        v = kv_ref[:, MEM_WIDTH + h * MEM_HEAD_DIM:MEM_WIDTH + (h + 1) * MEM_HEAD_DIM]
        s = lax.dot_general(q, k, (((1,), (1,)), ((), ())), preferred_element_type=F32) * scale
        mx = jnp.max(s, axis=-1, keepdims=True)
        p = jnp.exp(s - mx)
        den = jnp.sum(p, axis=-1, keepdims=True)
        o = jnp.dot(p.astype(BF), v, preferred_element_type=F32)
        o_ref[:, cols] = (o / den).astype(BF)


def _mem_attn(z_rest, kv, q_gain, batch_bounds, tm=512):
    m_tok = z_rest.shape[0]
    n_mem = kv.shape[1]
    q_block = (2 * SGU_WIDTH) // MEM_WIDTH

    def batch_of(i):
        b = 0
        for bound in batch_bounds[1:-1]:
            b = b + (i * tm >= bound).astype(jnp.int32)
        return b

    return pl.pallas_call(
        _mem_attn_kernel,
        grid=(m_tok // tm,),
        in_specs=[pl.BlockSpec((tm, MEM_WIDTH), lambda i: (i, q_block)),
                  pl.BlockSpec((None, n_mem, 2 * MEM_WIDTH), lambda i: (batch_of(i), 0, 0)),
                  pl.BlockSpec((1, MEM_HEAD_DIM), lambda i: (0, 0))],
        out_specs=pl.BlockSpec((tm, MEM_WIDTH), lambda i: (i, 0)),
        out_shape=jax.ShapeDtypeStruct((m_tok, MEM_WIDTH), BF),
        compiler_params=_params(1),
        name="mem_attn",
    )(z_rest, kv, q_gain)


def _merge_kernel(oa0_ref, oa1_ref, oa2_ref, l0_ref, l1_ref, l2_ref, ob_ref, om_ref,
                  g0_ref, g1_ref, g2_ref, wa_ref, wb_ref, wm_ref, o_ref, oa_scr, lse_scr, *, tm):
    n_h = HEADS_PER_GROUP
    n_g = len(ATTN_GROUPS)
    for g, (oa_ref, l_ref, (_, d)) in enumerate(zip((oa0_ref, oa1_ref, oa2_ref),
                                                    (l0_ref, l1_ref, l2_ref), ATTN_GROUPS)):
        for r in range(d):
            rows = pl.ds(r, tm // d, stride=d) if d > 1 else slice(None)
            o = oa_ref[r].astype(F32)
            for h in range(n_h):
                oa_scr[g * n_h + h, rows, :] = o[:, h * HEAD_DIM:(h + 1) * HEAD_DIM]
            lse_scr[g, rows, :] = l_ref[r]
    lses = [lse_scr[g] for g in range(n_g)]
    mx = jnp.maximum(jnp.maximum(lses[0], lses[1]), lses[2])
    es = [jnp.exp(l - mx) for l in lses]
    den = es[0] + es[1] + es[2]
    slabs = []
    for g in range(n_g):
        alpha = es[g] / den
        slabs += [(oa_scr[g * n_h + h] * alpha[:, h:h + 1]).astype(BF) for h in range(n_h)]
    pa = jnp.dot(jnp.concatenate(slabs, axis=1), wa_ref[...], preferred_element_type=F32)
    merged = jax.nn.sigmoid(g0_ref[...].astype(F32)) * pa
    pb = jnp.dot(ob_ref[...], wb_ref[...], preferred_element_type=F32)
    merged = merged + jax.nn.sigmoid(g1_ref[...].astype(F32)) * pb
    pm = jnp.dot(om_ref[...], wm_ref[...], preferred_element_type=F32)
    merged = merged + jax.nn.sigmoid(g2_ref[...].astype(F32)) * pm
    o_ref[...] = merged.astype(BF)


def _resident(shape):
    return pl.BlockSpec(shape, lambda i: (0,) * len(shape), pipeline_mode=pl.Buffered(1))


def _merge(oas, lses, ob, om, z_rest, wa, wb, wm, dm, tm=512):
    m_tok = ob.shape[0]
    gate0 = (2 * SGU_WIDTH + MEM_WIDTH) // dm
    oa_specs = [pl.BlockSpec((d, tm // d, GROUP_WIDTH), lambda i: (0, i, 0)) for _, d in ATTN_GROUPS]
    l_specs = [pl.BlockSpec((d, tm // d, LANES), lambda i: (0, i, 0)) for _, d in ATTN_GROUPS]
    g_specs = [pl.BlockSpec((tm, dm), lambda i, b=b: (i, gate0 + b)) for b in range(3)]
    return pl.pallas_call(
        functools.partial(_merge_kernel, tm=tm),
        grid=(m_tok // tm,),
        in_specs=oa_specs + l_specs
        + [pl.BlockSpec((tm, SGU_WIDTH), lambda i: (i, 0)), pl.BlockSpec((tm, MEM_WIDTH), lambda i: (i, 0))]
        + g_specs + [_resident(wa.shape), _resident(wb.shape), _resident(wm.shape)],
        out_specs=pl.BlockSpec((tm, dm), lambda i: (i, 0)),
        out_shape=jax.ShapeDtypeStruct((m_tok, dm), BF),
        scratch_shapes=[pltpu.VMEM((len(ATTN_GROUPS) * HEADS_PER_GROUP, tm, HEAD_DIM), F32),
                        pltpu.VMEM((len(ATTN_GROUPS), tm, LANES), F32)],
        compiler_params=_params(1),
        name="merge",
    )(*oas, *lses, ob, om, z_rest, z_rest, z_rest, wa, wb, wm)


def _out_kernel(mg_ref, xp_in, xs_in, wo_ref, g2_ref, wr_ref, xp_ref, xs_ref, h2_ref, lg_ref, *,
                n_prompt_tiles):
    i = pl.program_id(0)
    tm = mg_ref.shape[0]
    halves = [slice(u * (tm // 2), (u + 1) * (tm // 2)) for u in range(2)]
    nt = (((1,), (1,)), ((), ()))

    def run(x_in, x_out):
        wr = wr_ref[...]
        w_hi = wr.astype(BF)
        w_lo = (wr - w_hi.astype(F32)).astype(BF)
        proj = [jnp.dot(mg_ref[r, :], wo_ref[...], preferred_element_type=F32) for r in halves]
        x2 = [x_in[r, :] + p for r, p in zip(halves, proj)]
        for r, v in zip(halves, x2):
            x_out[r, :] = v
        h2 = [_rms(v, g2_ref[...]) for v in x2]
        for r, v in zip(halves, h2):
            h2_ref[r, :] = v
        for r, v in zip(halves, h2):
            h_hi = v.astype(BF)
            h_lo = (v - h_hi.astype(F32)).astype(BF)
            lg = lax.dot_general(w_hi, h_hi, nt, preferred_element_type=F32)
            lg = lg + lax.dot_general(w_hi, h_lo, nt, preferred_element_type=F32)
            lg = lg + lax.dot_general(w_lo, h_hi, nt, preferred_element_type=F32)
            lg_ref[:, r] = lg

    @pl.when(i < n_prompt_tiles)
    def _():
        run(xp_in, xp_ref)

    @pl.when(i >= n_prompt_tiles)
    def _():
        run(xs_in, xs_ref)


def _out_proj(merged, xp, xs, wo, g2, w_router_t, tm=256):
    n_prompt, dm = xp.shape
    m_tok = n_prompt + xs.shape[0]
    n_e = w_router_t.shape[0]
    npt = n_prompt // tm
    return pl.pallas_call(
        functools.partial(_out_kernel, n_prompt_tiles=npt),
        grid=(m_tok // tm,),
        in_specs=[pl.BlockSpec((tm, dm), lambda i: (i, 0)),
                  pl.BlockSpec((tm, dm), lambda i: (jnp.minimum(i, npt - 1), 0)),
                  pl.BlockSpec((tm, dm), lambda i: (jnp.maximum(i - npt, 0), 0)),
                  _resident(wo.shape),
                  pl.BlockSpec((1, dm), lambda i: (0, 0)),
                  pl.BlockSpec((n_e, dm), lambda i: (0, 0))],
        out_specs=[pl.BlockSpec((tm, dm), lambda i: (jnp.minimum(i, npt - 1), 0)),
                   pl.BlockSpec((tm, dm), lambda i: (jnp.maximum(i - npt, 0), 0)),
                   pl.BlockSpec((tm, dm), lambda i: (i, 0)),
                   pl.BlockSpec((n_e, tm), lambda i: (0, i))],
        out_shape=[jax.ShapeDtypeStruct((n_prompt, dm), F32),
                   jax.ShapeDtypeStruct((m_tok - n_prompt, dm), F32),
                   jax.ShapeDtypeStruct((m_tok, dm), F32),
                   jax.ShapeDtypeStruct((n_e, m_tok), F32)],
        compiler_params=_params(1),
        name="out_proj",
    )(merged, xp, xs, wo, g2, w_router_t)


def _router_kernel(lg_ref, u_ref, bl_ref, ui_ref, idx_ref, gate_ref, *, cap):
    n_e, n_r, _ = lg_ref.shape
    lg = lg_ref[...]
    ex = jnp.exp(lg - jnp.max(lg, axis=0, keepdims=True))
    aff = ex / jnp.sum(ex, axis=0, keepdims=True)
    key = lax.bitcast_convert_type(aff, jnp.int32)

    def count(mask):
        return jnp.sum(jnp.sum(mask.astype(F32), axis=2, keepdims=True), axis=1, keepdims=True)

    thr = jnp.zeros((n_e, 1, 1), jnp.int32)
    for bit in range(30, -1, -1):
        cand = thr | jnp.int32(1 << bit)
        thr = jnp.where(count(key >= cand) >= cap, cand, thr)
    gt = key > thr
    eq = key == thr
    need = cap - count(gt)

    def prefix(mask_f32):
        m2 = mask_f32.reshape(n_e * n_r, LANES)
        incl = jnp.dot(m2.astype(BF), u_ref[...], preferred_element_type=F32)
        tot = jnp.broadcast_to(incl[:, LANES - 1:LANES], incl.shape)
        rowoff = jnp.dot(bl_ref[...], tot.astype(BF), preferred_element_type=F32)
        return rowoff + incl - m2, incl

    eq_rank, _ = prefix(eq.astype(F32))
    take_eq = eq & (eq_rank.reshape(n_e, n_r, LANES) < need)
    sel = jnp.where(gt, 1.0, jnp.where(take_eq, 1.0, 0.0))
    _, incl = prefix(sel)
    sel2 = sel.reshape(n_e * n_r, LANES).astype(BF)

    def to_row(col):
        return jnp.transpose(jnp.broadcast_to(col, (cap, LANES)))[0:1, :]

    ones8 = jnp.ones((8, LANES), BF)
    slot = lax.broadcasted_iota(jnp.int32, (cap, 1), 0).astype(F32)
    row_id = lax.broadcasted_iota(jnp.int32, (cap, n_r), 1).astype(F32)
    lane_id = lax.broadcasted_iota(jnp.int32, (cap, LANES), 1).astype(F32)
    nt = (((1,), (1,)), ((), ()))
    for e in range(n_e):
        rows = slice(e * n_r, (e + 1) * n_r)
        tot_row = lax.dot_general(ones8, sel2[rows], nt, preferred_element_type=F32)[0:1]
        cum_row = jnp.dot(tot_row.astype(BF), ui_ref[...], preferred_element_type=F32)
        r_s = jnp.sum((cum_row <= slot).astype(F32), axis=1, keepdims=True)
        onehot = row_id == r_s
        oh = jnp.where(onehot, 1.0, 0.0).astype(BF)
        row_start = jnp.sum(jnp.where(onehot, cum_row - tot_row, 0.0), axis=1, keepdims=True)
        g_incl = jnp.dot(oh, incl[rows].astype(BF), preferred_element_type=F32)
        l_s = jnp.sum((g_incl <= slot - row_start).astype(F32), axis=1, keepdims=True)
        idx_ref[e:e + 1, :] = to_row(r_s * LANES + l_s).astype(jnp.int32)
        a = aff[e]
        a1 = a.astype(BF)
        a2 = (a - a1.astype(F32)).astype(BF)
        a3 = (a - a1.astype(F32) - a2.astype(F32)).astype(BF)
        g_aff = (jnp.dot(oh, a1, preferred_element_type=F32)
                 + jnp.dot(oh, a2, preferred_element_type=F32)) + jnp.dot(oh, a3, preferred_element_type=F32)
        gate_ref[e:e + 1, :] = to_row(jnp.sum(jnp.where(lane_id == l_s, g_aff, 0.0), axis=1, keepdims=True))


def _route(logits, cap):
    n_g, n_e, n_r, _ = logits.shape
    a = np.arange(LANES)
    upper = jnp.asarray(a[:, None] <= a[None, :], BF)
    i = np.arange(n_e * n_r)
    block_lower = jnp.asarray((i[:, None] // n_r == i[None, :] // n_r) & (i[None, :] < i[:, None]), BF)
    r = np.arange(n_r)
    upper_r = jnp.asarray(r[:, None] <= r[None, :], BF)
    return pl.pallas_call(
        functools.partial(_router_kernel, cap=cap),
        grid=(n_g,),
        in_specs=[pl.BlockSpec((None, n_e, n_r, LANES), lambda g: (g, 0, 0, 0)),
                  pl.BlockSpec((LANES, LANES), lambda g: (0, 0)),
                  pl.BlockSpec((n_e * n_r, n_e * n_r), lambda g: (0, 0)),
                  pl.BlockSpec((n_r, n_r), lambda g: (0, 0))],
        out_specs=[pl.BlockSpec((None, n_e, cap), lambda g: (g, 0, 0)),
                   pl.BlockSpec((None, n_e, cap), lambda g: (g, 0, 0))],
        out_shape=[jax.ShapeDtypeStruct((n_g, n_e, cap), jnp.int32),
                   jax.ShapeDtypeStruct((n_g, n_e, cap), F32)],
        compiler_params=_params(1),
        name="router",
    )(logits, upper, block_lower, upper_r)


def _expert_kernel(idx_ref, h2_hbm, gate_ref, wg_ref, wu_ref, wd_ref, xp_in, xs_in,
                   yp_hbm, ys_hbm, xb, stage, acc, sem_in, sem_out, *, cap, n_group_tok, chunk):
    del xp_in, xs_in
    e = pl.program_id(0)
    f = pl.program_id(1)
    n_f = pl.num_programs(1)
    n_slots = 2 * cap
    n_chunks = n_slots // chunk

    def row_copies(hbm, c, tok_off, to_vmem):
        base = e * n_slots + c * chunk
        buf = stage.at[c % 2]
        sem = (sem_in if to_vmem else sem_out).at[c % 2]

        def body(i, carry):
            s0 = pl.multiple_of(i * ROW_DMA_UNROLL, ROW_DMA_UNROLL)
            for j in range(ROW_DMA_UNROLL):
                t = idx_ref[base + s0 + j] - tok_off
                src, dst = hbm.at[pl.ds(t, 1)], buf.at[pl.ds(s0 + j, 1)]
                if not to_vmem:
                    src, dst = dst, src
                pltpu.make_async_copy(src, dst, sem).start()
            return carry
        lax.fori_loop(0, chunk // ROW_DMA_UNROLL, body, 0)

    def wait_rows(hbm, c, to_vmem):
        src, dst = hbm.at[pl.ds(0, chunk)], stage.at[c % 2]
        sem = (sem_in if to_vmem else sem_out).at[c % 2]
        if not to_vmem:
            src, dst = dst, src
        pltpu.make_async_copy(src, dst, sem).wait()

    @pl.when(f == 0)
    def _():
        row_copies(h2_hbm, 0, 0, True)
        for c in range(n_chunks):
            if c + 1 < n_chunks:
                row_copies(h2_hbm, c + 1, 0, True)
            wait_rows(h2_hbm, c, True)
            xb[c * chunk:(c + 1) * chunk, :] = stage[c % 2].astype(BF)
        acc[...] = jnp.zeros_like(acc)

    x = xb[...]
    a = jnp.dot(x, wg_ref[...].astype(BF), preferred_element_type=F32)
    b = jnp.dot(x, wu_ref[...].astype(BF), preferred_element_type=F32)
    hmid = (a * jax.nn.sigmoid(a) * b).astype(BF)
    dm = acc.shape[1]
    for n in range(dm // DOWN_TILE):
        cols = slice(n * DOWN_TILE, (n + 1) * DOWN_TILE)
        acc[:, cols] += jnp.dot(hmid, wd_ref[:, cols].astype(BF), preferred_element_type=F32)

    @pl.when(f == n_f - 1)
    def _():
        def target(c):
            group = (c * chunk) // cap
            return (yp_hbm, ys_hbm)[group], group * n_group_tok

        y_hbm, tok_off = target(0)
        row_copies(y_hbm, 0, tok_off, True)
        for c in range(n_chunks):
            if c + 1 < n_chunks:
                if c >= 1:
                    wait_rows(target(c - 1)[0], c - 1, False)
                y_hbm, tok_off = target(c + 1)
                row_copies(y_hbm, c + 1, tok_off, True)
            y_hbm, tok_off = target(c)
            wait_rows(y_hbm, c, True)
            rows = slice(c * chunk, (c + 1) * chunk)
            gate_col = jnp.transpose(jnp.broadcast_to(gate_ref[:, rows], (LANES, chunk)))[:, 0:1]
            stage[c % 2] = stage[c % 2] + acc[rows, :] * gate_col
            row_copies(y_hbm, c, tok_off, False)
        for c in range(max(n_chunks - 2, 0), n_chunks):
            wait_rows(target(c)[0], c, False)


def _experts(idx_flat, h2, gate, w_gate, w_up, w_down, x2p, x2s, cap, tf=256):
    n_e, dm, ff = w_gate.shape
    n_slots = 2 * cap
    chunk = min(GATHER_CHUNK, cap)
    n_group_tok = x2p.shape[0]
    any_spec = pl.BlockSpec(memory_space=pl.ANY)
    grid_spec = pltpu.PrefetchScalarGridSpec(
        num_scalar_prefetch=1,
        grid=(n_e, ff // tf),
        in_specs=[any_spec,
                  pl.BlockSpec((None, 1, n_slots), lambda e, f, idx: (e, 0, 0)),
                  pl.BlockSpec((None, dm, tf), lambda e, f, idx: (e, 0, f)),
                  pl.BlockSpec((None, dm, tf), lambda e, f, idx: (e, 0, f)),
                  pl.BlockSpec((None, tf, dm), lambda e, f, idx: (e, f, 0)),
                  any_spec, any_spec],
        out_specs=[any_spec, any_spec],
        scratch_shapes=[pltpu.VMEM((n_slots, dm), BF),
                        pltpu.VMEM((2, chunk, dm), F32),
                        pltpu.VMEM((n_slots, dm), F32),
                        pltpu.SemaphoreType.DMA((2,)),
                        pltpu.SemaphoreType.DMA((2,))],
    )
    return pl.pallas_call(
        functools.partial(_expert_kernel, cap=cap, n_group_tok=n_group_tok, chunk=chunk),
        grid_spec=grid_spec,
        out_shape=[jax.ShapeDtypeStruct(x2p.shape, F32), jax.ShapeDtypeStruct(x2s.shape, F32)],
        input_output_aliases={6: 0, 7: 1},
        compiler_params=_params(2),
        name="experts",
    )(idx_flat, h2, gate, w_gate, w_up, w_down, x2p, x2s)


def _rope_tables(seq_lens):
    half = ROPE_DIM // 2
    inv = ROPE_THETA ** (-jnp.arange(0, ROPE_DIM, 2, dtype=F32) / ROPE_DIM)
    pos = jnp.concatenate([jnp.arange(s, dtype=F32) for s in seq_lens])
    ang = pos[:, None] * inv[None, :]
    cos, sin = jnp.cos(ang), jnp.sin(ang)
    n = pos.shape[0]
    cos_t = jnp.concatenate([cos, cos, jnp.ones((n, HEAD_DIM - 2 * half), F32)], axis=1)
    sin_t = jnp.concatenate([-sin, sin, jnp.zeros((n, HEAD_DIM - 2 * half), F32)], axis=1)
    return cos_t, sin_t


def kernel(x_prompt, x_sample, mem_prompt, mem_sample, norm1_g, w_in, q_norm_a, k_norm_a, sgu_norm_g,
           sgu_w, sgu_b, mem_norm_g, w_mem_kv, q_norm_m, k_norm_m, w_proj_a, w_proj_b, w_proj_m, w_out,
           norm2_g, w_router, w_gate, w_up, w_down):
    assert norm1_g.shape[0] == 1, "one layer"
    bp, sp, dm = x_prompt.shape
    bs, ss, _ = x_sample.shape
    n_prompt, n_sample = bp * sp, bs * ss
    assert n_prompt == n_sample, "both request groups route the same number of tokens"
    n_e = w_router.shape[-1]
    cap = CAPACITY_FACTOR * n_prompt // n_e
    seq_lens = (sp,) * bp + (ss,) * bs
    seq_bounds = tuple(int(v) for v in np.concatenate([[0], np.cumsum(seq_lens)]))

    xp = x_prompt.reshape(n_prompt, dm)
    xs = x_sample.reshape(n_sample, dm)
    mem = jnp.concatenate([mem_prompt, mem_sample], axis=0)
    cos_t, sin_t = _rope_tables(seq_lens)
    qk_gain = jnp.concatenate([q_norm_a, k_norm_a], axis=0)

    def layer0(w):
        return w.reshape(w.shape[1:])

    w_in, sgu_w, sgu_b, w_mem_kv, w_proj_a, w_proj_b, w_proj_m, w_out, w_router, w_gate, w_up, w_down = map(
        layer0, (w_in, sgu_w, sgu_b, w_mem_kv, w_proj_a, w_proj_b, w_proj_m, w_out, w_router, w_gate, w_up,
                 w_down))

    h = _norm1(xp, xs, norm1_g)
    qkvs = _inproj_attn(h, w_in, qk_gain, cos_t, sin_t)
    z_rest = _inproj_rest(h, w_in, 3 * ATTN_WIDTH)

    oas, lses = [], []
    for qkv, (_, d) in zip(qkvs, ATTN_GROUPS):
        o, lse = _band_attention(qkv, d, seq_bounds)
        oas.append(o)
        lses.append(lse)
    ob = _sgu(z_rest, sgu_norm_g, sgu_w, sgu_b.T)
    kv = _mem_kv(mem, mem_norm_g, w_mem_kv, k_norm_m)
    om = _mem_attn(z_rest, kv, q_norm_m, seq_bounds)

    merged = _merge(oas, lses, ob, om, z_rest, _cast_bf16(w_proj_a), _cast_bf16(w_proj_b),
                    _cast_bf16(w_proj_m), dm)
    x2p, x2s, h2, logits_t = _out_proj(merged, xp, xs, _cast_bf16(w_out), norm2_g, w_router.T)

    logits = logits_t.reshape(n_e, 2, n_prompt // LANES, LANES).transpose(1, 0, 2, 3)
    idx, gate = _route(logits, cap)
    tok_off = jnp.arange(2, dtype=jnp.int32)[:, None, None] * n_prompt
    idx_flat = (idx + tok_off).transpose(1, 0, 2).reshape(-1)
    gate = gate.transpose(1, 0, 2).reshape(n_e, 1, 2 * cap)

    yp, ys = _experts(idx_flat, h2, gate, w_gate, w_up, w_down, x2p, x2s, cap)
    return yp.reshape(bp, sp, dm), ys.reshape(bs, ss, dm)
```

```python
import functools
import math

import numpy as np
import jax
import jax.numpy as jnp
from jax import lax
from jax.experimental import pallas as pl
from jax.experimental.pallas import tpu as pltpu

BF = jnp.bfloat16
F32 = jnp.float32

EPS = 1e-6
NEG = -1e30
LANES = 128
HEAD_DIM = 128
ATTN_GROUPS = ((128, 1), (512, 4), (2048, 16))
HEADS_PER_GROUP = 4
GROUP_WIDTH = HEADS_PER_GROUP * HEAD_DIM
ATTN_WIDTH = len(ATTN_GROUPS) * GROUP_WIDTH
BAND = 64
ROPE_THETA = 500000.0
ROPE_DIM = HEAD_DIM // 4
SGU_CHUNK = 128
SGU_GROUPS = 4
SGU_WIDTH = 1536
SGU_GROUP_WIDTH = SGU_WIDTH // SGU_GROUPS
MEM_HEADS = 4
MEM_HEAD_DIM = 256
MEM_WIDTH = MEM_HEADS * MEM_HEAD_DIM
CAPACITY_FACTOR = 2
ATTN_SUB_BLOCK = 128
ATTN_Q_BLOCK = 256
INPROJ_ROW_SPLIT = 4
GATHER_CHUNK = 512
ROW_DMA_UNROLL = 8
DOWN_TILE = 256
VMEM_LIMIT = 52 * 1024 * 1024
EXPERT_VMEM_LIMIT = 60 * 1024 * 1024


def _params(n_axes, vmem=VMEM_LIMIT):
    return pltpu.CompilerParams(dimension_semantics=("arbitrary",) * n_axes, vmem_limit_bytes=vmem)


def _rms(x, gain):
    ms = jnp.mean(x * x, axis=-1, keepdims=True)
    return x * lax.rsqrt(ms + EPS) * gain


def _gelu_tanh(x):
    c = math.sqrt(2.0 / math.pi)
    return x * (0.5 * (1.0 + jnp.tanh(c * (x + 0.044715 * (x * x * x)))))


def _cast_kernel(x_ref, o_ref):
    o_ref[...] = x_ref[...].astype(BF)


def _cast_bf16(w, rows=512):
    r, c = w.shape
    rows = min(rows, r)
    return pl.pallas_call(
        _cast_kernel,
        grid=(r // rows,),
        in_specs=[pl.BlockSpec((rows, c), lambda i: (i, 0))],
        out_specs=pl.BlockSpec((rows, c), lambda i: (i, 0)),
        out_shape=jax.ShapeDtypeStruct((r, c), BF),
        compiler_params=_params(1),
        name="cast_bf16",
    )(w)


def _qk_norm_rope(acc, gain, cos_t, sin_t):
    lane = lax.broadcasted_iota(jnp.int32, (1, HEAD_DIM), 1)
    half = ROPE_DIM // 2
    outs = []
    for h in range(HEADS_PER_GROUP):
        y = _rms(acc[:, h * HEAD_DIM:(h + 1) * HEAD_DIM], gain)
        partner = jnp.where(lane < half, pltpu.roll(y, HEAD_DIM - half, 1), pltpu.roll(y, half, 1))
        outs.append(y * cos_t + partner * sin_t)
    return jnp.concatenate(outs, axis=1)


def _store_dilated(val, scr, o_ref, d, row0):
    n_rows = val.shape[0]
    if d == 1:
        o_ref[0, row0:row0 + n_rows, :] = val.astype(BF)
        return
    for h in range(HEADS_PER_GROUP):
        scr[h, row0:row0 + n_rows, :] = val[:, h * HEAD_DIM:(h + 1) * HEAD_DIM]
    for r in range(d):
        o_ref[r, row0 // d:(row0 + n_rows) // d, :] = jnp.concatenate(
            [scr[h, pl.ds(row0 + r, n_rows // d, stride=d), :] for h in range(HEADS_PER_GROUP)],
            axis=1).astype(BF)


def _norm1_kernel(xp_ref, xs_ref, g_ref, h_ref, *, n_prompt_tiles):
    i = pl.program_id(0)
    for x_ref, is_mine in ((xp_ref, i < n_prompt_tiles), (xs_ref, i >= n_prompt_tiles)):
        @pl.when(is_mine)
        def _(x_ref=x_ref):
            h_ref[...] = _rms(x_ref[...], g_ref[...]).astype(BF)


def _norm1(xp, xs, g1, tm=512):
    dm = xp.shape[1]
    m_tok = xp.shape[0] + xs.shape[0]
    npt = xp.shape[0] // tm
    return pl.pallas_call(
        functools.partial(_norm1_kernel, n_prompt_tiles=npt),
        grid=(m_tok // tm,),
        in_specs=[pl.BlockSpec((tm, dm), lambda i: (jnp.minimum(i, npt - 1), 0)),
                  pl.BlockSpec((tm, dm), lambda i: (jnp.maximum(i - npt, 0), 0)),
                  pl.BlockSpec((1, dm), lambda i: (0, 0))],
        out_specs=pl.BlockSpec((tm, dm), lambda i: (i, 0)),
        out_shape=jax.ShapeDtypeStruct((m_tok, dm), BF),
        compiler_params=_params(1),
        name="norm1",
    )(xp, xs, g1)


def _inproj_attn_kernel(h_ref, w_ref, qkg_ref, cos_ref, sin_ref, o0_ref, o1_ref, o2_ref, scr, *, tm):
    n = pl.program_id(1)
    group = n // 3
    which = n % 3
    w = w_ref[...].astype(BF)
    rq = tm // INPROJ_ROW_SPLIT

    def run(o_ref, d, rope):
        gain = qkg_ref[pl.ds(jnp.minimum(which, 1), 1), :]
        for b in range(INPROJ_ROW_SPLIT):
            rows = slice(b * rq, (b + 1) * rq)
            val = jnp.dot(h_ref[rows, :], w, preferred_element_type=F32)
            if rope:
                val = _qk_norm_rope(val, gain, cos_ref[rows, :], sin_ref[rows, :])
            _store_dilated(val, scr, o_ref, d, b * rq)

    for g, (o_ref, (_, d)) in enumerate(zip((o0_ref, o1_ref, o2_ref), ATTN_GROUPS)):
        @pl.when((group == g) & (which < 2))
        def _(o_ref=o_ref, d=d):
            run(o_ref, d, True)

        @pl.when((group == g) & (which == 2))
        def _(o_ref=o_ref, d=d):
            run(o_ref, d, False)


def _inproj_attn(h, w_in, qk_gain, cos_t, sin_t, seq_bounds, tm=1024):
    m_tok, dm = h.shape
    n_steps = 3 * len(ATTN_GROUPS)

    def pos_block(m):
        start = 0
        for bound in seq_bounds[1:-1]:
            start = jnp.where(m * tm >= bound, bound // tm, start)
        return m - start

    def out_spec(g, d):
        return pl.BlockSpec((None, d, tm // d, GROUP_WIDTH),
                            lambda m, n: (jnp.clip(n - 3 * g, 0, 2), 0, m, 0))

    return pl.pallas_call(
        functools.partial(_inproj_attn_kernel, tm=tm),
        grid=(m_tok // tm, n_steps),
        in_specs=[
            pl.BlockSpec((tm, dm), lambda m, n: (m, 0)),
            pl.BlockSpec((dm, GROUP_WIDTH), lambda m, n: (0, (n % 3) * 3 + n // 3)),
            pl.BlockSpec((2, HEAD_DIM), lambda m, n: (0, 0)),
            pl.BlockSpec((tm, HEAD_DIM), lambda m, n: (pos_block(m), 0)),
            pl.BlockSpec((tm, HEAD_DIM), lambda m, n: (pos_block(m), 0)),
        ],
        out_specs=[out_spec(g, d) for g, (_, d) in enumerate(ATTN_GROUPS)],
        out_shape=[jax.ShapeDtypeStruct((3, d, m_tok // d, GROUP_WIDTH), BF) for _, d in ATTN_GROUPS],
        scratch_shapes=[pltpu.VMEM((HEADS_PER_GROUP, tm, HEAD_DIM), F32)],
        compiler_params=_params(2),
        name="inproj_attn",
    )(h, w_in, qk_gain, cos_t, sin_t)


def _matmul_kernel(a_ref, w_ref, o_ref):
    o_ref[...] = jnp.dot(a_ref[...], w_ref[...].astype(BF),
                         preferred_element_type=F32).astype(o_ref.dtype)


def _inproj_rest(h, w_in, col0, tm=2048, tn=512):
    m_tok, dm = h.shape
    n_cols = w_in.shape[1] - col0
    tm = min(tm, m_tok)
    off = col0 // tn
    return pl.pallas_call(
        _matmul_kernel,
        grid=(m_tok // tm, n_cols // tn),
        in_specs=[pl.BlockSpec((tm, dm), lambda m, n: (m, 0)),
                  pl.BlockSpec((dm, tn), lambda m, n: (0, off + n))],
        out_specs=pl.BlockSpec((tm, tn), lambda m, n: (m, n)),
        out_shape=jax.ShapeDtypeStruct((m_tok, n_cols), BF),
        compiler_params=_params(2),
        name="inproj_rest",
    )(h, w_in)


def _attn_kernel(q_ref, kp_ref, kc_ref, kn_ref, vp_ref, vc_ref, vn_ref, o_ref, lse_ref, *, bounds, tq):
    sub = ATTN_SUB_BLOCK
    j0 = pl.program_id(1) * tq
    lo = jnp.int32(bounds[0])
    hi = jnp.int32(bounds[1])
    for b0, b1 in zip(bounds[1:-1], bounds[2:]):
        lo = jnp.where(j0 >= b0, b0, lo)
        hi = jnp.where(j0 >= b0, b1, hi)
    k_all = jnp.concatenate([kp_ref[...], kc_ref[...], kn_ref[...]], axis=0)
    v_all = jnp.concatenate([vp_ref[...], vc_ref[...], vn_ref[...]], axis=0)
    scale = 1.0 / math.sqrt(HEAD_DIM)
    lane = lax.broadcasted_iota(jnp.int32, (1, LANES), 1)
    n_keys = sub + 2 * BAND
    chains = [(u, h) for u in range(tq // sub) for h in range(HEADS_PER_GROUP)]

    def head(h):
        return slice(h * HEAD_DIM, (h + 1) * HEAD_DIM)

    def keys(u):
        return slice((u + 1) * sub - BAND, (u + 1) * sub - BAND + n_keys)

    valid = []
    for u in range(tq // sub):
        qpos = j0 + u * sub + lax.broadcasted_iota(jnp.int32, (sub, 1), 0)
        kpos = j0 + u * sub - BAND + lax.broadcasted_iota(jnp.int32, (1, n_keys), 1)
        valid.append((jnp.abs(kpos - qpos) <= BAND) & (kpos >= lo) & (kpos < hi))
    s = [lax.dot_general(q_ref[u * sub:(u + 1) * sub, head(h)], k_all[keys(u), head(h)],
                         (((1,), (1,)), ((), ())), preferred_element_type=F32) for u, h in chains]
    s = [jnp.where(valid[u], si * scale, NEG) for (u, h), si in zip(chains, s)]
    mx = [jnp.max(si, axis=-1, keepdims=True) for si in s]
    p = [jnp.exp(si - mi) for si, mi in zip(s, mx)]
    den = [jnp.sum(pi, axis=-1, keepdims=True) for pi in p]
    o = [jnp.dot(pi.astype(BF), v_all[keys(u), head(h)], preferred_element_type=F32)
         for (u, h), pi in zip(chains, p)]
    for u in range(tq // sub):
        rows = slice(u * sub, (u + 1) * sub)
        lse_all = jnp.zeros((sub, LANES), F32)
        for h in range(HEADS_PER_GROUP):
            c = chains.index((u, h))
            o_ref[rows, head(h)] = (o[c] / den[c]).astype(BF)
            lse_all = jnp.where(lane == h, mx[c] + jnp.log(den[c]), lse_all)
        lse_ref[rows, :] = lse_all


def _band_attention(qkv, d, seq_bounds):
    _, _, n_j, _ = qkv.shape
    sub = ATTN_SUB_BLOCK
    bounds = tuple(b // d for b in seq_bounds)
    tq = max(t for t in (sub, ATTN_Q_BLOCK) if all(b % t == 0 for b in bounds))
    n_blocks = n_j // tq
    n_sub = n_j // sub
    per = tq // sub

    def spec(which, shift):
        if shift == 0:
            return pl.BlockSpec((None, None, tq, GROUP_WIDTH), lambda r, i: (which, r, i, 0))
        halo = -1 if shift < 0 else per
        return pl.BlockSpec((None, None, sub, GROUP_WIDTH),
                            lambda r, i: (which, r, jnp.clip(i * per + halo, 0, n_sub - 1), 0))

    return pl.pallas_call(
        functools.partial(_attn_kernel, bounds=bounds, tq=tq),
        grid=(d, n_blocks),
        in_specs=[spec(0, 0), spec(1, -1), spec(1, 0), spec(1, 1), spec(2, -1), spec(2, 0), spec(2, 1)],
        out_specs=[pl.BlockSpec((None, tq, GROUP_WIDTH), lambda r, i: (r, i, 0)),
                   pl.BlockSpec((None, tq, LANES), lambda r, i: (r, i, 0))],
        out_shape=[jax.ShapeDtypeStruct((d, n_j, GROUP_WIDTH), BF),
                   jax.ShapeDtypeStruct((d, n_j, LANES), F32)],
        compiler_params=_params(2),
        name=f"band_attention_d{d}",
    )(qkv, qkv, qkv, qkv, qkv, qkv, qkv)


def _sgu_kernel(u_ref, v_ref, ng_ref, ws_ref, bs_ref, o_ref, *, tb):
    vn = _rms(_gelu_tanh(v_ref[...].astype(F32)), ng_ref[...]).astype(BF)
    for g in range(SGU_GROUPS):
        cols = slice(g * SGU_GROUP_WIDTH, (g + 1) * SGU_GROUP_WIDTH)
        w = ws_ref[g].astype(BF)
        bias = bs_ref[:, g:g + 1]
        for c in range(tb // SGU_CHUNK):
            rows = slice(c * SGU_CHUNK, (c + 1) * SGU_CHUNK)
            mixed = jnp.dot(w, vn[rows, cols], preferred_element_type=F32) + bias
            u = _gelu_tanh(u_ref[rows, cols].astype(F32))
            o_ref[rows, cols] = (u * mixed).astype(BF)


def _sgu(z_rest, norm_g, w_s, b_s_t, tb=512):
    m_tok = z_rest.shape[0]
    return pl.pallas_call(
        functools.partial(_sgu_kernel, tb=tb),
        grid=(m_tok // tb,),
        in_specs=[pl.BlockSpec((tb, SGU_WIDTH), lambda i: (i, 0)),
                  pl.BlockSpec((tb, SGU_WIDTH), lambda i: (i, 1)),
                  pl.BlockSpec((1, SGU_WIDTH), lambda i: (0, 0)),
                  pl.BlockSpec((SGU_GROUPS, SGU_CHUNK, SGU_CHUNK), lambda i: (0, 0, 0)),
                  pl.BlockSpec((SGU_CHUNK, SGU_GROUPS), lambda i: (0, 0))],
        out_specs=pl.BlockSpec((tb, SGU_WIDTH), lambda i: (i, 0)),
        out_shape=jax.ShapeDtypeStruct((m_tok, SGU_WIDTH), BF),
        compiler_params=_params(1),
        name="sgu",
    )(z_rest, z_rest, norm_g, w_s, b_s_t)


def _mem_kv_kernel(mem_ref, ng_ref, w_ref, kg_ref, o_ref, *, n_k_tiles, tn):
    n = pl.program_id(1)
    hm = _rms(mem_ref[...], ng_ref[...]).astype(BF)
    kv = jnp.dot(hm, w_ref[...].astype(BF), preferred_element_type=F32)

    @pl.when(n < n_k_tiles)
    def _():
        for h in range(tn // MEM_HEAD_DIM):
            cols = slice(h * MEM_HEAD_DIM, (h + 1) * MEM_HEAD_DIM)
            o_ref[:, cols] = _rms(kv[:, cols], kg_ref[...]).astype(BF)

    @pl.when(n >= n_k_tiles)
    def _():
        o_ref[...] = kv.astype(BF)


def _mem_kv(mem, norm_g, w_kv, k_gain, tn=512):
    n_b, n_mem, dm = mem.shape
    width = w_kv.shape[1]
    return pl.pallas_call(
        functools.partial(_mem_kv_kernel, n_k_tiles=MEM_WIDTH // tn, tn=tn),
        grid=(n_b, width // tn),
        in_specs=[pl.BlockSpec((None, n_mem, dm), lambda b, n: (b, 0, 0)),
                  pl.BlockSpec((1, dm), lambda b, n: (0, 0)),
                  pl.BlockSpec((dm, tn), lambda b, n: (0, n)),
                  pl.BlockSpec((1, MEM_HEAD_DIM), lambda b, n: (0, 0))],
        out_specs=pl.BlockSpec((None, n_mem, tn), lambda b, n: (b, 0, n)),
        out_shape=jax.ShapeDtypeStruct((n_b, n_mem, width), BF),
        compiler_params=_params(2),
        name="mem_kv",
    )(mem, norm_g, w_kv, k_gain)


def _mem_attn_kernel(q_ref, kv_ref, qg_ref, o_ref):
    scale = 1.0 / math.sqrt(MEM_HEAD_DIM)
    for h in range(MEM_HEADS):
        cols = slice(h * MEM_HEAD_DIM, (h + 1) * MEM_HEAD_DIM)
        q = _rms(q_ref[:, cols].astype(F32), qg_ref[...]).astype(BF)
        k = kv_ref[:, cols]
        v = kv_ref[:, MEM_WIDTH + h * MEM_HEAD_DIM:MEM_WIDTH + (h + 1) * MEM_HEAD_DIM]
        s = lax.dot_general(q, k, (((1,), (1,)), ((), ())), preferred_element_type=F32) * scale
        mx = jnp.max(s, axis=-1, keepdims=True)
        p = jnp.exp(s - mx)
        den = jnp.sum(p, axis=-1, keepdims=True)
        o = jnp.dot(p.astype(BF), v, preferred_element_type=F32)
        o_ref[:, cols] = (o / den).astype(BF)


def _mem_attn(z_rest, kv, q_gain, batch_bounds, tm=512):
    m_tok = z_rest.shape[0]
    n_mem = kv.shape[1]
    q_block = (2 * SGU_WIDTH) // MEM_WIDTH

    def batch_of(i):
        b = 0
        for bound in batch_bounds[1:-1]:
            b = b + (i * tm >= bound).astype(jnp.int32)
        return b

    return pl.pallas_call(
        _mem_attn_kernel,
        grid=(m_tok // tm,),
        in_specs=[pl.BlockSpec((tm, MEM_WIDTH), lambda i: (i, q_block)),
                  pl.BlockSpec((None, n_mem, 2 * MEM_WIDTH), lambda i: (batch_of(i), 0, 0)),
                  pl.BlockSpec((1, MEM_HEAD_DIM), lambda i: (0, 0))],
        out_specs=pl.BlockSpec((tm, MEM_WIDTH), lambda i: (i, 0)),
        out_shape=jax.ShapeDtypeStruct((m_tok, MEM_WIDTH), BF),
        compiler_params=_params(1),
        name="mem_attn",
    )(z_rest, kv, q_gain)


def _merge_kernel(oa0_ref, oa1_ref, oa2_ref, l0_ref, l1_ref, l2_ref, ob_ref, om_ref,
                  g0_ref, g1_ref, g2_ref, wa_ref, wb_ref, wm_ref, o_ref, oa_scr, lse_scr, *, tm):
    n_h = HEADS_PER_GROUP
    n_g = len(ATTN_GROUPS)
    for g, (oa_ref, l_ref, (_, d)) in enumerate(zip((oa0_ref, oa1_ref, oa2_ref),
                                                    (l0_ref, l1_ref, l2_ref), ATTN_GROUPS)):
        for r in range(d):
            rows = pl.ds(r, tm // d, stride=d) if d > 1 else slice(None)
            o = oa_ref[r].astype(F32)
            for h in range(n_h):
                oa_scr[g * n_h + h, rows, :] = o[:, h * HEAD_DIM:(h + 1) * HEAD_DIM]
            lse_scr[g, rows, :] = l_ref[r]
    lses = [lse_scr[g] for g in range(n_g)]
    mx = jnp.maximum(jnp.maximum(lses[0], lses[1]), lses[2])
    es = [jnp.exp(l - mx) for l in lses]
    den = es[0] + es[1] + es[2]
    slabs = []
    for g in range(n_g):
        alpha = es[g] / den
        slabs += [(oa_scr[g * n_h + h] * alpha[:, h:h + 1]).astype(BF) for h in range(n_h)]
    pa = jnp.dot(jnp.concatenate(slabs, axis=1), wa_ref[...], preferred_element_type=F32)
    merged = jax.nn.sigmoid(g0_ref[...].astype(F32)) * pa
    pb = jnp.dot(ob_ref[...], wb_ref[...], preferred_element_type=F32)
    merged = merged + jax.nn.sigmoid(g1_ref[...].astype(F32)) * pb
    pm = jnp.dot(om_ref[...], wm_ref[...], preferred_element_type=F32)
    merged = merged + jax.nn.sigmoid(g2_ref[...].astype(F32)) * pm
    o_ref[...] = merged.astype(BF)


def _resident(shape):
    return pl.BlockSpec(shape, lambda i: (0,) * len(shape), pipeline_mode=pl.Buffered(1))


def _merge(oas, lses, ob, om, z_rest, wa, wb, wm, dm, tm=512):
    m_tok = ob.shape[0]
    gate0 = (2 * SGU_WIDTH + MEM_WIDTH) // dm
    oa_specs = [pl.BlockSpec((d, tm // d, GROUP_WIDTH), lambda i: (0, i, 0)) for _, d in ATTN_GROUPS]
    l_specs = [pl.BlockSpec((d, tm // d, LANES), lambda i: (0, i, 0)) for _, d in ATTN_GROUPS]
    g_specs = [pl.BlockSpec((tm, dm), lambda i, b=b: (i, gate0 + b)) for b in range(3)]
    return pl.pallas_call(
        functools.partial(_merge_kernel, tm=tm),
        grid=(m_tok // tm,),
        in_specs=oa_specs + l_specs
        + [pl.BlockSpec((tm, SGU_WIDTH), lambda i: (i, 0)), pl.BlockSpec((tm, MEM_WIDTH), lambda i: (i, 0))]
        + g_specs + [_resident(wa.shape), _resident(wb.shape), _resident(wm.shape)],
        out_specs=pl.BlockSpec((tm, dm), lambda i: (i, 0)),
        out_shape=jax.ShapeDtypeStruct((m_tok, dm), BF),
        scratch_shapes=[pltpu.VMEM((len(ATTN_GROUPS) * HEADS_PER_GROUP, tm, HEAD_DIM), F32),
                        pltpu.VMEM((len(ATTN_GROUPS), tm, LANES), F32)],
        compiler_params=_params(1),
        name="merge",
    )(*oas, *lses, ob, om, z_rest, z_rest, z_rest, wa, wb, wm)


def _out_kernel(mg_ref, xp_in, xs_in, wo_ref, g2_ref, wr_ref, xp_ref, xs_ref, h2_ref, lg_ref, *,
                n_prompt_tiles):
    i = pl.program_id(0)
    tm = mg_ref.shape[0]
    halves = [slice(u * (tm // 2), (u + 1) * (tm // 2)) for u in range(2)]
    nt = (((1,), (1,)), ((), ()))

    def run(x_in, x_out):
        wr = wr_ref[...]
        w_hi = wr.astype(BF)
        w_lo = (wr - w_hi.astype(F32)).astype(BF)
        proj = [jnp.dot(mg_ref[r, :], wo_ref[...], preferred_element_type=F32) for r in halves]
        x2 = [x_in[r, :] + p for r, p in zip(halves, proj)]
        for r, v in zip(halves, x2):
            x_out[r, :] = v
        h2 = [_rms(v, g2_ref[...]) for v in x2]
        for r, v in zip(halves, h2):
            bits = lax.bitcast_convert_type(v.astype(BF).astype(F32), jnp.uint32)
            pack_rows = v.shape[1] // (2 * LANES)
            for j in range(pack_rows):
                lo = bits[:, 2 * j * LANES:(2 * j + 1) * LANES] >> 16
                hi = bits[:, (2 * j + 1) * LANES:(2 * j + 2) * LANES] & jnp.uint32(0xFFFF0000)
                h2_ref[pl.ds(r.start * pack_rows + j, r.stop - r.start, stride=pack_rows), :] = lo | hi
        for r, v in zip(halves, h2):
            h_hi = v.astype(BF)
            h_lo = (v - h_hi.astype(F32)).astype(BF)
            lg = lax.dot_general(w_hi, h_hi, nt, preferred_element_type=F32)
            lg = lg + lax.dot_general(w_hi, h_lo, nt, preferred_element_type=F32)
            lg = lg + lax.dot_general(w_lo, h_hi, nt, preferred_element_type=F32)
            lg_ref[:, r] = lg

    @pl.when(i < n_prompt_tiles)
    def _():
        run(xp_in, xp_ref)

    @pl.when(i >= n_prompt_tiles)
    def _():
        run(xs_in, xs_ref)


def _out_proj(merged, xp, xs, wo, g2, w_router_t, tm=256):
    n_prompt, dm = xp.shape
    m_tok = n_prompt + xs.shape[0]
    n_e = w_router_t.shape[0]
    npt = n_prompt // tm
    pack_rows = dm // (2 * LANES)
    return pl.pallas_call(
        functools.partial(_out_kernel, n_prompt_tiles=npt),
        grid=(m_tok // tm,),
        in_specs=[pl.BlockSpec((tm, dm), lambda i: (i, 0)),
                  pl.BlockSpec((tm, dm), lambda i: (jnp.minimum(i, npt - 1), 0)),
                  pl.BlockSpec((tm, dm), lambda i: (jnp.maximum(i - npt, 0), 0)),
                  _resident(wo.shape),
                  pl.BlockSpec((1, dm), lambda i: (0, 0)),
                  pl.BlockSpec((n_e, dm), lambda i: (0, 0))],
        out_specs=[pl.BlockSpec((tm, dm), lambda i: (jnp.minimum(i, npt - 1), 0)),
                   pl.BlockSpec((tm, dm), lambda i: (jnp.maximum(i - npt, 0), 0)),
                   pl.BlockSpec((tm * pack_rows, LANES), lambda i: (i, 0)),
                   pl.BlockSpec((n_e, tm), lambda i: (0, i))],
        out_shape=[jax.ShapeDtypeStruct((n_prompt, dm), F32),
                   jax.ShapeDtypeStruct((m_tok - n_prompt, dm), F32),
                   jax.ShapeDtypeStruct((m_tok * pack_rows, LANES), jnp.uint32),
                   jax.ShapeDtypeStruct((n_e, m_tok), F32)],
        compiler_params=_params(1),
        name="out_proj",
    )(merged, xp, xs, wo, g2, w_router_t)


def _router_kernel(lg_ref, u_ref, bl_ref, ui_ref, idx_ref, gate_ref, *, cap):
    n_e, n_r, _ = lg_ref.shape
    lg = lg_ref[...]
    ex = jnp.exp(lg - jnp.max(lg, axis=0, keepdims=True))
    aff = ex / jnp.sum(ex, axis=0, keepdims=True)
    key = lax.bitcast_convert_type(aff, jnp.int32)

    def count(mask):
        return jnp.sum(jnp.sum(mask.astype(F32), axis=2, keepdims=True), axis=1, keepdims=True)

    thr = jnp.zeros((n_e, 1, 1), jnp.int32)
    for bit in range(30, -1, -1):
        cand = thr | jnp.int32(1 << bit)
        thr = jnp.where(count(key >= cand) >= cap, cand, thr)
    gt = key > thr
    eq = key == thr
    need = cap - count(gt)

    def prefix(mask_f32):
        m2 = mask_f32.reshape(n_e * n_r, LANES)
        incl = jnp.dot(m2.astype(BF), u_ref[...], preferred_element_type=F32)
        tot = jnp.broadcast_to(incl[:, LANES - 1:LANES], incl.shape)
        rowoff = jnp.dot(bl_ref[...], tot.astype(BF), preferred_element_type=F32)
        return rowoff + incl - m2, incl

    eq_rank, _ = prefix(eq.astype(F32))
    take_eq = eq & (eq_rank.reshape(n_e, n_r, LANES) < need)
    sel = jnp.where(gt, 1.0, jnp.where(take_eq, 1.0, 0.0))
    _, incl = prefix(sel)
    sel2 = sel.reshape(n_e * n_r, LANES).astype(BF)

    def to_row(col):
        return jnp.transpose(jnp.broadcast_to(col, (cap, LANES)))[0:1, :]

    ones8 = jnp.ones((8, LANES), BF)
    slot = lax.broadcasted_iota(jnp.int32, (cap, 1), 0).astype(F32)
    row_id = lax.broadcasted_iota(jnp.int32, (cap, n_r), 1).astype(F32)
    lane_id = lax.broadcasted_iota(jnp.int32, (cap, LANES), 1).astype(F32)
    nt = (((1,), (1,)), ((), ()))
    for e in range(n_e):
        rows = slice(e * n_r, (e + 1) * n_r)
        tot_row = lax.dot_general(ones8, sel2[rows], nt, preferred_element_type=F32)[0:1]
        cum_row = jnp.dot(tot_row.astype(BF), ui_ref[...], preferred_element_type=F32)
        r_s = jnp.sum((cum_row <= slot).astype(F32), axis=1, keepdims=True)
        onehot = row_id == r_s
        oh = jnp.where(onehot, 1.0, 0.0).astype(BF)
        row_start = jnp.sum(jnp.where(onehot, cum_row - tot_row, 0.0), axis=1, keepdims=True)
        g_incl = jnp.dot(oh, incl[rows].astype(BF), preferred_element_type=F32)
        l_s = jnp.sum((g_incl <= slot - row_start).astype(F32), axis=1, keepdims=True)
        idx_ref[e:e + 1, :] = to_row(r_s * LANES + l_s).astype(jnp.int32)
        a = aff[e]
        a1 = a.astype(BF)
        a2 = (a - a1.astype(F32)).astype(BF)
        a3 = (a - a1.astype(F32) - a2.astype(F32)).astype(BF)
        g_aff = (jnp.dot(oh, a1, preferred_element_type=F32)
                 + jnp.dot(oh, a2, preferred_element_type=F32)) + jnp.dot(oh, a3, preferred_element_type=F32)
        gate_ref[e:e + 1, :] = to_row(jnp.sum(jnp.where(lane_id == l_s, g_aff, 0.0), axis=1, keepdims=True))


def _route(logits, cap):
    n_g, n_e, n_r, _ = logits.shape
    a = np.arange(LANES)
    upper = jnp.asarray(a[:, None] <= a[None, :], BF)
    i = np.arange(n_e * n_r)
    block_lower = jnp.asarray((i[:, None] // n_r == i[None, :] // n_r) & (i[None, :] < i[:, None]), BF)
    r = np.arange(n_r)
    upper_r = jnp.asarray(r[:, None] <= r[None, :], BF)
    return pl.pallas_call(
        functools.partial(_router_kernel, cap=cap),
        grid=(n_g,),
        in_specs=[pl.BlockSpec((None, n_e, n_r, LANES), lambda g: (g, 0, 0, 0)),
                  pl.BlockSpec((LANES, LANES), lambda g: (0, 0)),
                  pl.BlockSpec((n_e * n_r, n_e * n_r), lambda g: (0, 0)),
                  pl.BlockSpec((n_r, n_r), lambda g: (0, 0))],
        out_specs=[pl.BlockSpec((None, n_e, cap), lambda g: (g, 0, 0)),
                   pl.BlockSpec((None, n_e, cap), lambda g: (g, 0, 0))],
        out_shape=[jax.ShapeDtypeStruct((n_g, n_e, cap), jnp.int32),
                   jax.ShapeDtypeStruct((n_g, n_e, cap), F32)],
        compiler_params=_params(1),
        name="router",
    )(logits, upper, block_lower, upper_r)


def _expert_kernel(idx_ref, h2_hbm, gate_ref, wg_ref, wu_ref, wd_ref, xp_in, xs_in,
                   yp_hbm, ys_hbm, xb, xg, stage, acc, sem_x, sem_in, sem_out, *,
                   cap, n_group_tok, chunk, n_f, rows_per_step):
    del xp_in, xs_in
    e = pl.program_id(0)
    f = pl.program_id(1)
    n_e = pl.num_programs(0)
    n_slots = 2 * cap
    n_chunks = n_slots // chunk
    pack_rows = xg.shape[0] // (n_f * rows_per_step)

    def x_row_copy(expert, s):
        t = idx_ref[expert * n_slots + jnp.minimum(s, n_slots - 1)]
        src = h2_hbm.at[pl.ds(pl.multiple_of(t * pack_rows, pack_rows), pack_rows)]
        dst = xg.at[pl.ds(pl.multiple_of(s * pack_rows, pack_rows), pack_rows)]
        pltpu.make_async_copy(src, dst, sem_x).start()

    def x_rows_wait(n_rows):
        pltpu.make_async_copy(h2_hbm.at[pl.ds(0, n_rows * pack_rows)],
                              xg.at[pl.ds(0, n_rows * pack_rows)], sem_x).wait()

    @pl.when((f == 0) & (e == 0))
    def _():
        def body(i, carry):
            for j in range(ROW_DMA_UNROLL):
                x_row_copy(0, i * ROW_DMA_UNROLL + j)
            return carry
        lax.fori_loop(0, n_slots // ROW_DMA_UNROLL, body, 0)
        x_rows_wait(n_slots)

    @pl.when((f == 0) & (e > 0))
    def _():
        for _ in range(n_f):
            x_rows_wait(rows_per_step)

    @pl.when(f == 0)
    def _():
        for c in range(n_chunks):
            for j in range(pack_rows):
                w = xg[pl.ds(c * chunk * pack_rows + j, chunk, stride=pack_rows), :]
                lo = lax.bitcast_convert_type(w << 16, F32).astype(BF)
                hi = lax.bitcast_convert_type(w & jnp.uint32(0xFFFF0000), F32).astype(BF)
                xb[c * chunk:(c + 1) * chunk, 2 * j * LANES:(2 * j + 1) * LANES] = lo
                xb[c * chunk:(c + 1) * chunk, (2 * j + 1) * LANES:(2 * j + 2) * LANES] = hi
        acc[...] = jnp.zeros_like(acc)

    nxt = jnp.minimum(e + 1, n_e - 1)
    for j in range(rows_per_step):
        x_row_copy(nxt, f * rows_per_step + j)

    def row_copies(hbm, c, tok_off, to_vmem):
        base = e * n_slots + c * chunk
        buf = stage.at[c % 2]
        sem = (sem_in if to_vmem else sem_out).at[c % 2]

        def body(i, carry):
            s0 = pl.multiple_of(i * ROW_DMA_UNROLL, ROW_DMA_UNROLL)
            for j in range(ROW_DMA_UNROLL):
                t = idx_ref[base + s0 + j] - tok_off
                src, dst = hbm.at[pl.ds(t, 1)], buf.at[pl.ds(s0 + j, 1)]
                if not to_vmem:
                    src, dst = dst, src
                pltpu.make_async_copy(src, dst, sem).start()
            return carry
        lax.fori_loop(0, chunk // ROW_DMA_UNROLL, body, 0)

    def wait_rows(hbm, c, to_vmem):
        src, dst = hbm.at[pl.ds(0, chunk)], stage.at[c % 2]
        sem = (sem_in if to_vmem else sem_out).at[c % 2]
        if not to_vmem:
            src, dst = dst, src
        pltpu.make_async_copy(src, dst, sem).wait()

    x = xb[...]
    a = jnp.dot(x, wg_ref[...].astype(BF), preferred_element_type=F32)
    b = jnp.dot(x, wu_ref[...].astype(BF), preferred_element_type=F32)
    hmid = (a * jax.nn.sigmoid(a) * b).astype(BF)
    dm = acc.shape[1]
    for n in range(dm // DOWN_TILE):
        cols = slice(n * DOWN_TILE, (n + 1) * DOWN_TILE)
        acc[:, cols] += jnp.dot(hmid, wd_ref[:, cols].astype(BF), preferred_element_type=F32)

    @pl.when(f == n_f - 1)
    def _():
        def target(c):
            group = (c * chunk) // cap
            return (yp_hbm, ys_hbm)[group], group * n_group_tok

        y_hbm, tok_off = target(0)
        row_copies(y_hbm, 0, tok_off, True)
        for c in range(n_chunks):
            if c + 1 < n_chunks:
                if c >= 1:
                    wait_rows(target(c - 1)[0], c - 1, False)
                y_hbm, tok_off = target(c + 1)
                row_copies(y_hbm, c + 1, tok_off, True)
            y_hbm, tok_off = target(c)
            wait_rows(y_hbm, c, True)
            rows = slice(c * chunk, (c + 1) * chunk)
            gate_col = jnp.transpose(jnp.broadcast_to(gate_ref[:, rows], (LANES, chunk)))[:, 0:1]
            stage[c % 2] = stage[c % 2] + acc[rows, :] * gate_col
            row_copies(y_hbm, c, tok_off, False)
        for c in range(max(n_chunks - 2, 0), n_chunks):
            wait_rows(target(c)[0], c, False)

    @pl.when((f == n_f - 1) & (e == n_e - 1))
    def _():
        for _ in range(n_f):
            x_rows_wait(rows_per_step)


def _experts(idx_flat, h2, gate, w_gate, w_up, w_down, x2p, x2s, cap, tf=256):
    n_e, dm, ff = w_gate.shape
    n_slots = 2 * cap
    chunk = min(GATHER_CHUNK, cap)
    n_group_tok = x2p.shape[0]
    n_f = ff // tf
    rows_per_step = -(-n_slots // n_f)
    pack_rows = dm // (2 * LANES)
    any_spec = pl.BlockSpec(memory_space=pl.ANY)
    grid_spec = pltpu.PrefetchScalarGridSpec(
        num_scalar_prefetch=1,
        grid=(n_e, n_f),
        in_specs=[any_spec,
                  pl.BlockSpec((None, 1, n_slots), lambda e, f, idx: (e, 0, 0)),
                  pl.BlockSpec((None, dm, tf), lambda e, f, idx: (e, 0, f)),
                  pl.BlockSpec((None, dm, tf), lambda e, f, idx: (e, 0, f)),
                  pl.BlockSpec((None, tf, dm), lambda e, f, idx: (e, f, 0)),
                  any_spec, any_spec],
        out_specs=[any_spec, any_spec],
        scratch_shapes=[pltpu.VMEM((n_slots, dm), BF),
                        pltpu.VMEM((n_f * rows_per_step * pack_rows, LANES), jnp.uint32),
                        pltpu.VMEM((2, chunk, dm), F32),
                        pltpu.VMEM((n_slots, dm), F32),
                        pltpu.SemaphoreType.DMA(()),
                        pltpu.SemaphoreType.DMA((2,)),
                        pltpu.SemaphoreType.DMA((2,))],
    )
    return pl.pallas_call(
        functools.partial(_expert_kernel, cap=cap, n_group_tok=n_group_tok, chunk=chunk, n_f=n_f,
                          rows_per_step=rows_per_step),
        grid_spec=grid_spec,
        out_shape=[jax.ShapeDtypeStruct(x2p.shape, F32), jax.ShapeDtypeStruct(x2s.shape, F32)],
        input_output_aliases={6: 0, 7: 1},
        compiler_params=_params(2, EXPERT_VMEM_LIMIT),
        name="experts",
    )(idx_flat, h2, gate, w_gate, w_up, w_down, x2p, x2s)


def _rope_tables(max_len):
    half = ROPE_DIM // 2
    inv = ROPE_THETA ** (-np.arange(0, ROPE_DIM, 2, dtype=np.float64) / ROPE_DIM)
    ang = np.arange(max_len, dtype=np.float64)[:, None] * inv[None, :]
    cos, sin = np.cos(ang), np.sin(ang)
    cos_t = np.concatenate([cos, cos, np.ones((max_len, HEAD_DIM - 2 * half))], axis=1)
    sin_t = np.concatenate([-sin, sin, np.zeros((max_len, HEAD_DIM - 2 * half))], axis=1)
    return jnp.asarray(cos_t, F32), jnp.asarray(sin_t, F32)


def kernel(x_prompt, x_sample, mem_prompt, mem_sample, norm1_g, w_in, q_norm_a, k_norm_a, sgu_norm_g,
           sgu_w, sgu_b, mem_norm_g, w_mem_kv, q_norm_m, k_norm_m, w_proj_a, w_proj_b, w_proj_m, w_out,
           norm2_g, w_router, w_gate, w_up, w_down):
    assert norm1_g.shape[0] == 1, "one layer"
    bp, sp, dm = x_prompt.shape
    bs, ss, _ = x_sample.shape
    n_prompt, n_sample = bp * sp, bs * ss
    assert n_prompt == n_sample, "both request groups route the same number of tokens"
    n_e = w_router.shape[-1]
    cap = CAPACITY_FACTOR * n_prompt // n_e
    seq_lens = (sp,) * bp + (ss,) * bs
    seq_bounds = tuple(int(v) for v in np.concatenate([[0], np.cumsum(seq_lens)]))

    xp = x_prompt.reshape(n_prompt, dm)
    xs = x_sample.reshape(n_sample, dm)
    mem = jnp.concatenate([mem_prompt, mem_sample], axis=0)
    cos_t, sin_t = _rope_tables(max(seq_lens))
    qk_gain = jnp.concatenate([q_norm_a, k_norm_a], axis=0)

    def layer0(w):
        return w.reshape(w.shape[1:])

    w_in, sgu_w, sgu_b, w_mem_kv, w_proj_a, w_proj_b, w_proj_m, w_out, w_router, w_gate, w_up, w_down = map(
        layer0, (w_in, sgu_w, sgu_b, w_mem_kv, w_proj_a, w_proj_b, w_proj_m, w_out, w_router, w_gate, w_up,
                 w_down))

    h = _norm1(xp, xs, norm1_g)
    qkvs = _inproj_attn(h, w_in, qk_gain, cos_t, sin_t, seq_bounds)
    z_rest = _inproj_rest(h, w_in, 3 * ATTN_WIDTH)

    oas, lses = [], []
    for qkv, (_, d) in zip(qkvs, ATTN_GROUPS):
        o, lse = _band_attention(qkv, d, seq_bounds)
        oas.append(o)
        lses.append(lse)
    ob = _sgu(z_rest, sgu_norm_g, sgu_w, sgu_b.T)
    kv = _mem_kv(mem, mem_norm_g, w_mem_kv, k_norm_m)
    om = _mem_attn(z_rest, kv, q_norm_m, seq_bounds)

    merged = _merge(oas, lses, ob, om, z_rest, _cast_bf16(w_proj_a), _cast_bf16(w_proj_b),
                    _cast_bf16(w_proj_m), dm)
    x2p, x2s, h2, logits_t = _out_proj(merged, xp, xs, _cast_bf16(w_out), norm2_g, w_router.T)

    logits = logits_t.reshape(n_e, 2, n_prompt // LANES, LANES).transpose(1, 0, 2, 3)
    idx, gate = _route(logits, cap)
    tok_off = jnp.arange(2, dtype=jnp.int32)[:, None, None] * n_prompt
    idx_flat = (idx + tok_off).transpose(1, 0, 2).reshape(-1)
    gate = gate.transpose(1, 0, 2).reshape(n_e, 1, 2 * cap)

    yp, ys = _experts(idx_flat, h2, gate, w_gate, w_up, w_down, x2p, x2s, cap)
    return yp.reshape(bp, sp, dm), ys.reshape(bs, ss, dm)
```

```python
import functools
import math

import numpy as np
import jax
import jax.numpy as jnp
from jax import lax
from jax.experimental import pallas as pl
from jax.experimental.pallas import tpu as pltpu

BF = jnp.bfloat16
F32 = jnp.float32

EPS = 1e-6
NEG = -1e30
LANES = 128
HEAD_DIM = 128
ATTN_GROUPS = ((128, 1), (512, 4), (2048, 16))
HEADS_PER_GROUP = 4
GROUP_WIDTH = HEADS_PER_GROUP * HEAD_DIM
ATTN_WIDTH = len(ATTN_GROUPS) * GROUP_WIDTH
BAND = 64
ROPE_THETA = 500000.0
ROPE_DIM = HEAD_DIM // 4
SGU_CHUNK = 128
SGU_GROUPS = 4
SGU_WIDTH = 1536
SGU_GROUP_WIDTH = SGU_WIDTH // SGU_GROUPS
MEM_HEADS = 4
MEM_HEAD_DIM = 256
MEM_WIDTH = MEM_HEADS * MEM_HEAD_DIM
CAPACITY_FACTOR = 2
ATTN_SUB_BLOCK = 128
ATTN_Q_BLOCK = 512
INPROJ_ROW_SPLIT = 4
GATHER_CHUNK = 512
ROW_DMA_UNROLL = 8
DOWN_TILE = 256
VMEM_LIMIT = 52 * 1024 * 1024
EXPERT_VMEM_LIMIT = 60 * 1024 * 1024


def _params(n_axes, vmem=VMEM_LIMIT):
    return pltpu.CompilerParams(dimension_semantics=("arbitrary",) * n_axes, vmem_limit_bytes=vmem)


def _rms(x, gain):
    ms = jnp.mean(x * x, axis=-1, keepdims=True)
    return x * lax.rsqrt(ms + EPS) * gain


def _gelu_tanh(x):
    c = math.sqrt(2.0 / math.pi)
    return x * (0.5 * (1.0 + jnp.tanh(c * (x + 0.044715 * (x * x * x)))))


def _cast_kernel(x_ref, o_ref):
    o_ref[...] = x_ref[...].astype(BF)


def _cast_bf16(w, rows=512):
    r, c = w.shape
    rows = min(rows, r)
    return pl.pallas_call(
        _cast_kernel,
        grid=(r // rows,),
        in_specs=[pl.BlockSpec((rows, c), lambda i: (i, 0))],
        out_specs=pl.BlockSpec((rows, c), lambda i: (i, 0)),
        out_shape=jax.ShapeDtypeStruct((r, c), BF),
        compiler_params=_params(1),
        name="cast_bf16",
    )(w)


def _qk_norm_rope(acc, gain, cos_t, sin_t):
    lane = lax.broadcasted_iota(jnp.int32, (1, HEAD_DIM), 1)
    half = ROPE_DIM // 2
    outs = []
    for h in range(HEADS_PER_GROUP):
        y = _rms(acc[:, h * HEAD_DIM:(h + 1) * HEAD_DIM], gain)
        partner = jnp.where(lane < half, pltpu.roll(y, HEAD_DIM - half, 1), pltpu.roll(y, half, 1))
        outs.append(y * cos_t + partner * sin_t)
    return jnp.concatenate(outs, axis=1)


def _store_dilated(val, scr, o_ref, d, row0):
    n_rows = val.shape[0]
    if d == 1:
        o_ref[0, row0:row0 + n_rows, :] = val.astype(BF)
        return
    for h in range(HEADS_PER_GROUP):
        scr[h, row0:row0 + n_rows, :] = val[:, h * HEAD_DIM:(h + 1) * HEAD_DIM]
    for r in range(d):
        o_ref[r, row0 // d:(row0 + n_rows) // d, :] = jnp.concatenate(
            [scr[h, pl.ds(row0 + r, n_rows // d, stride=d), :] for h in range(HEADS_PER_GROUP)],
            axis=1).astype(BF)


def _norm1_kernel(xp_ref, xs_ref, g_ref, h_ref, *, n_prompt_tiles):
    i = pl.program_id(0)
    for x_ref, is_mine in ((xp_ref, i < n_prompt_tiles), (xs_ref, i >= n_prompt_tiles)):
        @pl.when(is_mine)
        def _(x_ref=x_ref):
            h_ref[...] = _rms(x_ref[...], g_ref[...]).astype(BF)


def _norm1(xp, xs, g1, tm=512):
    dm = xp.shape[1]
    m_tok = xp.shape[0] + xs.shape[0]
    npt = xp.shape[0] // tm
    return pl.pallas_call(
        functools.partial(_norm1_kernel, n_prompt_tiles=npt),
        grid=(m_tok // tm,),
        in_specs=[pl.BlockSpec((tm, dm), lambda i: (jnp.minimum(i, npt - 1), 0)),
                  pl.BlockSpec((tm, dm), lambda i: (jnp.maximum(i - npt, 0), 0)),
                  pl.BlockSpec((1, dm), lambda i: (0, 0))],
        out_specs=pl.BlockSpec((tm, dm), lambda i: (i, 0)),
        out_shape=jax.ShapeDtypeStruct((m_tok, dm), BF),
        compiler_params=_params(1),
        name="norm1",
    )(xp, xs, g1)


def _inproj_attn_kernel(h_ref, w_ref, qkg_ref, cos_ref, sin_ref, o0_ref, o1_ref, o2_ref, scr, *, tm):
    n = pl.program_id(1)
    group = n // 3
    which = n % 3
    w = w_ref[...].astype(BF)
    rq = tm // INPROJ_ROW_SPLIT

    def run(o_ref, d, rope):
        gain = qkg_ref[pl.ds(jnp.minimum(which, 1), 1), :]
        for b in range(INPROJ_ROW_SPLIT):
            rows = slice(b * rq, (b + 1) * rq)
            val = jnp.dot(h_ref[rows, :], w, preferred_element_type=F32)
            if rope:
                val = _qk_norm_rope(val, gain, cos_ref[rows, :], sin_ref[rows, :])
            _store_dilated(val, scr, o_ref, d, b * rq)

    for g, (o_ref, (_, d)) in enumerate(zip((o0_ref, o1_ref, o2_ref), ATTN_GROUPS)):
        @pl.when((group == g) & (which < 2))
        def _(o_ref=o_ref, d=d):
            run(o_ref, d, True)

        @pl.when((group == g) & (which == 2))
        def _(o_ref=o_ref, d=d):
            run(o_ref, d, False)


def _inproj_attn(h, w_in, qk_gain, cos_t, sin_t, seq_bounds, tm=1024):
    m_tok, dm = h.shape
    n_steps = 3 * len(ATTN_GROUPS)

    def pos_block(m):
        start = 0
        for bound in seq_bounds[1:-1]:
            start = jnp.where(m * tm >= bound, bound // tm, start)
        return m - start

    def out_spec(g, d):
        return pl.BlockSpec((None, d, tm // d, GROUP_WIDTH),
                            lambda m, n: (jnp.clip(n - 3 * g, 0, 2), 0, m, 0))

    return pl.pallas_call(
        functools.partial(_inproj_attn_kernel, tm=tm),
        grid=(m_tok // tm, n_steps),
        in_specs=[
            pl.BlockSpec((tm, dm), lambda m, n: (m, 0)),
            pl.BlockSpec((dm, GROUP_WIDTH), lambda m, n: (0, (n % 3) * 3 + n // 3)),
            pl.BlockSpec((2, HEAD_DIM), lambda m, n: (0, 0)),
            pl.BlockSpec((tm, HEAD_DIM), lambda m, n: (pos_block(m), 0)),
            pl.BlockSpec((tm, HEAD_DIM), lambda m, n: (pos_block(m), 0)),
        ],
        out_specs=[out_spec(g, d) for g, (_, d) in enumerate(ATTN_GROUPS)],
        out_shape=[jax.ShapeDtypeStruct((3, d, m_tok // d, GROUP_WIDTH), BF) for _, d in ATTN_GROUPS],
        scratch_shapes=[pltpu.VMEM((HEADS_PER_GROUP, tm, HEAD_DIM), F32)],
        compiler_params=_params(2),
        name="inproj_attn",
    )(h, w_in, qk_gain, cos_t, sin_t)


def _matmul_kernel(a_ref, w_ref, o_ref):
    o_ref[...] = jnp.dot(a_ref[...], w_ref[...].astype(BF),
                         preferred_element_type=F32).astype(o_ref.dtype)


def _inproj_rest(h, w_in, col0, tm=2048, tn=512):
    m_tok, dm = h.shape
    n_cols = w_in.shape[1] - col0
    tm = min(tm, m_tok)
    off = col0 // tn
    return pl.pallas_call(
        _matmul_kernel,
        grid=(m_tok // tm, n_cols // tn),
        in_specs=[pl.BlockSpec((tm, dm), lambda m, n: (m, 0)),
                  pl.BlockSpec((dm, tn), lambda m, n: (0, off + n))],
        out_specs=pl.BlockSpec((tm, tn), lambda m, n: (m, n)),
        out_shape=jax.ShapeDtypeStruct((m_tok, n_cols), BF),
        compiler_params=_params(2),
        name="inproj_rest",
    )(h, w_in)


def _attn_kernel(q_ref, kp_ref, kc_ref, kn_ref, vp_ref, vc_ref, vn_ref, o_ref, lse_ref, *, bounds, tq):
    sub = ATTN_SUB_BLOCK
    j0 = pl.program_id(1) * tq
    lo = jnp.int32(bounds[0])
    hi = jnp.int32(bounds[1])
    for b0, b1 in zip(bounds[1:-1], bounds[2:]):
        lo = jnp.where(j0 >= b0, b0, lo)
        hi = jnp.where(j0 >= b0, b1, hi)
    k_all = jnp.concatenate([kp_ref[...], kc_ref[...], kn_ref[...]], axis=0)
    v_all = jnp.concatenate([vp_ref[...], vc_ref[...], vn_ref[...]], axis=0)
    scale = 1.0 / math.sqrt(HEAD_DIM)
    lane = lax.broadcasted_iota(jnp.int32, (1, LANES), 1)
    n_keys = sub + 2 * BAND
    chains = [(u, h) for u in range(tq // sub) for h in range(HEADS_PER_GROUP)]

    def head(h):
        return slice(h * HEAD_DIM, (h + 1) * HEAD_DIM)

    def keys(u):
        return slice((u + 1) * sub - BAND, (u + 1) * sub - BAND + n_keys)

    valid = []
    for u in range(tq // sub):
        qpos = j0 + u * sub + lax.broadcasted_iota(jnp.int32, (sub, 1), 0)
        kpos = j0 + u * sub - BAND + lax.broadcasted_iota(jnp.int32, (1, n_keys), 1)
        valid.append((jnp.abs(kpos - qpos) <= BAND) & (kpos >= lo) & (kpos < hi))
    s = [lax.dot_general(q_ref[u * sub:(u + 1) * sub, head(h)], k_all[keys(u), head(h)],
                         (((1,), (1,)), ((), ())), preferred_element_type=F32) for u, h in chains]
    s = [jnp.where(valid[u], si * scale, NEG) for (u, h), si in zip(chains, s)]
    mx = [jnp.max(si, axis=-1, keepdims=True) for si in s]
    p = [jnp.exp(si - mi) for si, mi in zip(s, mx)]
    den = [jnp.sum(pi, axis=-1, keepdims=True) for pi in p]
    o = [jnp.dot(pi.astype(BF), v_all[keys(u), head(h)], preferred_element_type=F32)
         for (u, h), pi in zip(chains, p)]
    for u in range(tq // sub):
        rows = slice(u * sub, (u + 1) * sub)
        lse_all = jnp.zeros((sub, LANES), F32)
        for h in range(HEADS_PER_GROUP):
            c = chains.index((u, h))
            o_ref[rows, head(h)] = (o[c] / den[c]).astype(BF)
            lse_all = jnp.where(lane == h, mx[c] + jnp.log(den[c]), lse_all)
        lse_ref[rows, :] = lse_all


def _band_attention(qkv, d, seq_bounds):
    _, _, n_j, _ = qkv.shape
    sub = ATTN_SUB_BLOCK
    bounds = tuple(b // d for b in seq_bounds)
    tq = max(t for t in (sub, 2 * sub, ATTN_Q_BLOCK) if all(b % t == 0 for b in bounds))
    n_blocks = n_j // tq
    n_sub = n_j // sub
    per = tq // sub

    def spec(which, shift):
        if shift == 0:
            return pl.BlockSpec((None, None, tq, GROUP_WIDTH), lambda r, i: (which, r, i, 0))
        halo = -1 if shift < 0 else per
        return pl.BlockSpec((None, None, sub, GROUP_WIDTH),
                            lambda r, i: (which, r, jnp.clip(i * per + halo, 0, n_sub - 1), 0))

    return pl.pallas_call(
        functools.partial(_attn_kernel, bounds=bounds, tq=tq),
        grid=(d, n_blocks),
        in_specs=[spec(0, 0), spec(1, -1), spec(1, 0), spec(1, 1), spec(2, -1), spec(2, 0), spec(2, 1)],
        out_specs=[pl.BlockSpec((None, tq, GROUP_WIDTH), lambda r, i: (r, i, 0)),
                   pl.BlockSpec((None, tq, LANES), lambda r, i: (r, i, 0))],
        out_shape=[jax.ShapeDtypeStruct((d, n_j, GROUP_WIDTH), BF),
                   jax.ShapeDtypeStruct((d, n_j, LANES), F32)],
        compiler_params=_params(2),
        name=f"band_attention_d{d}",
    )(qkv, qkv, qkv, qkv, qkv, qkv, qkv)


def _sgu_kernel(u_ref, v_ref, ng_ref, ws_ref, bs_ref, o_ref, *, tb):
    vn = _rms(_gelu_tanh(v_ref[...].astype(F32)), ng_ref[...]).astype(BF)
    for g in range(SGU_GROUPS):
        cols = slice(g * SGU_GROUP_WIDTH, (g + 1) * SGU_GROUP_WIDTH)
        w = ws_ref[g].astype(BF)
        bias = bs_ref[:, g:g + 1]
        for c in range(tb // SGU_CHUNK):
            rows = slice(c * SGU_CHUNK, (c + 1) * SGU_CHUNK)
            mixed = jnp.dot(w, vn[rows, cols], preferred_element_type=F32) + bias
            u = _gelu_tanh(u_ref[rows, cols].astype(F32))
            o_ref[rows, cols] = (u * mixed).astype(BF)


def _sgu(z_rest, norm_g, w_s, b_s_t, tb=512):
    m_tok = z_rest.shape[0]
    return pl.pallas_call(
        functools.partial(_sgu_kernel, tb=tb),
        grid=(m_tok // tb,),
        in_specs=[pl.BlockSpec((tb, SGU_WIDTH), lambda i: (i, 0)),
                  pl.BlockSpec((tb, SGU_WIDTH), lambda i: (i, 1)),
                  pl.BlockSpec((1, SGU_WIDTH), lambda i: (0, 0)),
                  pl.BlockSpec((SGU_GROUPS, SGU_CHUNK, SGU_CHUNK), lambda i: (0, 0, 0)),
                  pl.BlockSpec((SGU_CHUNK, SGU_GROUPS), lambda i: (0, 0))],
        out_specs=pl.BlockSpec((tb, SGU_WIDTH), lambda i: (i, 0)),
        out_shape=jax.ShapeDtypeStruct((m_tok, SGU_WIDTH), BF),
        compiler_params=_params(1),
        name="sgu",
    )(z_rest, z_rest, norm_g, w_s, b_s_t)


def _mem_kv_kernel(mem_ref, ng_ref, w_ref, kg_ref, o_ref, *, n_k_tiles, tn):
    n = pl.program_id(1)
    hm = _rms(mem_ref[...], ng_ref[...]).astype(BF)
    kv = jnp.dot(hm, w_ref[...].astype(BF), preferred_element_type=F32)

    @pl.when(n < n_k_tiles)
    def _():
        for h in range(tn // MEM_HEAD_DIM):
            cols = slice(h * MEM_HEAD_DIM, (h + 1) * MEM_HEAD_DIM)
            o_ref[:, cols] = _rms(kv[:, cols], kg_ref[...]).astype(BF)

    @pl.when(n >= n_k_tiles)
    def _():
        o_ref[...] = kv.astype(BF)


def _mem_kv(mem, norm_g, w_kv, k_gain, tn=512):
    n_b, n_mem, dm = mem.shape
    width = w_kv.shape[1]
    return pl.pallas_call(
        functools.partial(_mem_kv_kernel, n_k_tiles=MEM_WIDTH // tn, tn=tn),
        grid=(n_b, width // tn),
        in_specs=[pl.BlockSpec((None, n_mem, dm), lambda b, n: (b, 0, 0)),
                  pl.BlockSpec((1, dm), lambda b, n: (0, 0)),
                  pl.BlockSpec((dm, tn), lambda b, n: (0, n)),
                  pl.BlockSpec((1, MEM_HEAD_DIM), lambda b, n: (0, 0))],
        out_specs=pl.BlockSpec((None, n_mem, tn), lambda b, n: (b, 0, n)),
        out_shape=jax.ShapeDtypeStruct((n_b, n_mem, width), BF),
        compiler_params=_params(2),
        name="mem_kv",
    )(mem, norm_g, w_kv, k_gain)


def _mem_attn_kernel(q_ref, kv_ref, qg_ref, o_ref):
    scale = 1.0 / math.sqrt(MEM_HEAD_DIM)
    heads = [slice(h * MEM_HEAD_DIM, (h + 1) * MEM_HEAD_DIM) for h in range(MEM_HEADS)]
    nt = (((1,), (1,)), ((), ()))
    q = [_rms(q_ref[:, c].astype(F32), qg_ref[...]).astype(BF) for c in heads]
    s = [lax.dot_general(qh, kv_ref[:, c], nt, preferred_element_type=F32) * scale for qh, c in zip(q, heads)]
    mx = [jnp.max(sh, axis=-1, keepdims=True) for sh in s]
    p = [jnp.exp(sh - mh) for sh, mh in zip(s, mx)]
    den = [jnp.sum(ph, axis=-1, keepdims=True) for ph in p]
    o = [jnp.dot(ph.astype(BF), kv_ref[:, MEM_WIDTH + c.start:MEM_WIDTH + c.stop], preferred_element_type=F32)
         for ph, c in zip(p, heads)]
    for c, oh, dh in zip(heads, o, den):
        o_ref[:, c] = (oh / dh).astype(BF)


def _mem_attn(z_rest, kv, q_gain, batch_bounds, tm=512):
    m_tok = z_rest.shape[0]
    n_mem = kv.shape[1]
    q_block = (2 * SGU_WIDTH) // MEM_WIDTH

    def batch_of(i):
        b = 0
        for bound in batch_bounds[1:-1]:
            b = b + (i * tm >= bound).astype(jnp.int32)
        return b

    return pl.pallas_call(
        _mem_attn_kernel,
        grid=(m_tok // tm,),
        in_specs=[pl.BlockSpec((tm, MEM_WIDTH), lambda i: (i, q_block)),
                  pl.BlockSpec((None, n_mem, 2 * MEM_WIDTH), lambda i: (batch_of(i), 0, 0)),
                  pl.BlockSpec((1, MEM_HEAD_DIM), lambda i: (0, 0))],
        out_specs=pl.BlockSpec((tm, MEM_WIDTH), lambda i: (i, 0)),
        out_shape=jax.ShapeDtypeStruct((m_tok, MEM_WIDTH), BF),
        compiler_params=_params(1),
        name="mem_attn",
    )(z_rest, kv, q_gain)


def _merge_kernel(oa0_ref, oa1_ref, oa2_ref, l0_ref, l1_ref, l2_ref, ob_ref, om_ref,
                  g0_ref, g1_ref, g2_ref, wa_ref, wb_ref, wm_ref, o_ref, oa_scr, lse_scr, *, tm):
    n_h = HEADS_PER_GROUP
    n_g = len(ATTN_GROUPS)
    for g, (oa_ref, l_ref, (_, d)) in enumerate(zip((oa0_ref, oa1_ref, oa2_ref),
                                                    (l0_ref, l1_ref, l2_ref), ATTN_GROUPS)):
        for r in range(d):
            rows = pl.ds(r, tm // d, stride=d) if d > 1 else slice(None)
            o = oa_ref[r].astype(F32)
            for h in range(n_h):
                oa_scr[g * n_h + h, rows, :] = o[:, h * HEAD_DIM:(h + 1) * HEAD_DIM]
            lse_scr[g, rows, :] = l_ref[r]
    lses = [lse_scr[g] for g in range(n_g)]
    mx = jnp.maximum(jnp.maximum(lses[0], lses[1]), lses[2])
    es = [jnp.exp(l - mx) for l in lses]
    den = es[0] + es[1] + es[2]
    slabs = []
    for g in range(n_g):
        alpha = es[g] / den
        slabs += [(oa_scr[g * n_h + h] * alpha[:, h:h + 1]).astype(BF) for h in range(n_h)]
    pa = jnp.dot(jnp.concatenate(slabs, axis=1), wa_ref[...], preferred_element_type=F32)
    merged = jax.nn.sigmoid(g0_ref[...].astype(F32)) * pa
    pb = jnp.dot(ob_ref[...], wb_ref[...], preferred_element_type=F32)
    merged = merged + jax.nn.sigmoid(g1_ref[...].astype(F32)) * pb
    pm = jnp.dot(om_ref[...], wm_ref[...], preferred_element_type=F32)
    merged = merged + jax.nn.sigmoid(g2_ref[...].astype(F32)) * pm
    o_ref[...] = merged.astype(BF)


def _resident(shape):
    return pl.BlockSpec(shape, lambda i: (0,) * len(shape), pipeline_mode=pl.Buffered(1))


def _merge(oas, lses, ob, om, z_rest, wa, wb, wm, dm, tm=512):
    m_tok = ob.shape[0]
    gate0 = (2 * SGU_WIDTH + MEM_WIDTH) // dm
    oa_specs = [pl.BlockSpec((d, tm // d, GROUP_WIDTH), lambda i: (0, i, 0)) for _, d in ATTN_GROUPS]
    l_specs = [pl.BlockSpec((d, tm // d, LANES), lambda i: (0, i, 0)) for _, d in ATTN_GROUPS]
    g_specs = [pl.BlockSpec((tm, dm), lambda i, b=b: (i, gate0 + b)) for b in range(3)]
    return pl.pallas_call(
        functools.partial(_merge_kernel, tm=tm),
        grid=(m_tok // tm,),
        in_specs=oa_specs + l_specs
        + [pl.BlockSpec((tm, SGU_WIDTH), lambda i: (i, 0)), pl.BlockSpec((tm, MEM_WIDTH), lambda i: (i, 0))]
        + g_specs + [_resident(wa.shape), _resident(wb.shape), _resident(wm.shape)],
        out_specs=pl.BlockSpec((tm, dm), lambda i: (i, 0)),
        out_shape=jax.ShapeDtypeStruct((m_tok, dm), BF),
        scratch_shapes=[pltpu.VMEM((len(ATTN_GROUPS) * HEADS_PER_GROUP, tm, HEAD_DIM), F32),
                        pltpu.VMEM((len(ATTN_GROUPS), tm, LANES), F32)],
        compiler_params=_params(1),
        name="merge",
    )(*oas, *lses, ob, om, z_rest, z_rest, z_rest, wa, wb, wm)


def _out_kernel(mg_ref, xp_in, xs_in, wo_ref, g2_ref, wr_ref, xp_ref, xs_ref, h2_ref, lg_ref, *,
                n_prompt_tiles):
    i = pl.program_id(0)
    tm = mg_ref.shape[0]
    halves = [slice(u * (tm // 2), (u + 1) * (tm // 2)) for u in range(2)]
    nt = (((1,), (1,)), ((), ()))

    def run(x_in, x_out):
        wr = wr_ref[...]
        w_hi = wr.astype(BF)
        w_lo = (wr - w_hi.astype(F32)).astype(BF)
        proj = [jnp.dot(mg_ref[r, :], wo_ref[...], preferred_element_type=F32) for r in halves]
        x2 = [x_in[r, :] + p for r, p in zip(halves, proj)]
        for r, v in zip(halves, x2):
            x_out[r, :] = v
        h2 = [_rms(v, g2_ref[...]) for v in x2]
        for r, v in zip(halves, h2):
            bits = lax.bitcast_convert_type(v.astype(BF).astype(F32), jnp.uint32)
            pack_rows = v.shape[1] // (2 * LANES)
            for j in range(pack_rows):
                lo = bits[:, 2 * j * LANES:(2 * j + 1) * LANES] >> 16
                hi = bits[:, (2 * j + 1) * LANES:(2 * j + 2) * LANES] & jnp.uint32(0xFFFF0000)
                h2_ref[pl.ds(r.start * pack_rows + j, r.stop - r.start, stride=pack_rows), :] = lo | hi
        for r, v in zip(halves, h2):
            h_hi = v.astype(BF)
            h_lo = (v - h_hi.astype(F32)).astype(BF)
            lg = lax.dot_general(w_hi, h_hi, nt, preferred_element_type=F32)
            lg = lg + lax.dot_general(w_hi, h_lo, nt, preferred_element_type=F32)
            lg = lg + lax.dot_general(w_lo, h_hi, nt, preferred_element_type=F32)
            lg_ref[:, r] = lg

    @pl.when(i < n_prompt_tiles)
    def _():
        run(xp_in, xp_ref)

    @pl.when(i >= n_prompt_tiles)
    def _():
        run(xs_in, xs_ref)


def _out_proj(merged, xp, xs, wo, g2, w_router_t, tm=256):
    n_prompt, dm = xp.shape
    m_tok = n_prompt + xs.shape[0]
    n_e = w_router_t.shape[0]
    npt = n_prompt // tm
    pack_rows = dm // (2 * LANES)
    return pl.pallas_call(
        functools.partial(_out_kernel, n_prompt_tiles=npt),
        grid=(m_tok // tm,),
        in_specs=[pl.BlockSpec((tm, dm), lambda i: (i, 0)),
                  pl.BlockSpec((tm, dm), lambda i: (jnp.minimum(i, npt - 1), 0)),
                  pl.BlockSpec((tm, dm), lambda i: (jnp.maximum(i - npt, 0), 0)),
                  _resident(wo.shape),
                  pl.BlockSpec((1, dm), lambda i: (0, 0)),
                  pl.BlockSpec((n_e, dm), lambda i: (0, 0))],
        out_specs=[pl.BlockSpec((tm, dm), lambda i: (jnp.minimum(i, npt - 1), 0)),
                   pl.BlockSpec((tm, dm), lambda i: (jnp.maximum(i - npt, 0), 0)),
                   pl.BlockSpec((tm * pack_rows, LANES), lambda i: (i, 0)),
                   pl.BlockSpec((n_e, tm), lambda i: (0, i))],
        out_shape=[jax.ShapeDtypeStruct((n_prompt, dm), F32),
                   jax.ShapeDtypeStruct((m_tok - n_prompt, dm), F32),
                   jax.ShapeDtypeStruct((m_tok * pack_rows, LANES), jnp.uint32),
                   jax.ShapeDtypeStruct((n_e, m_tok), F32)],
        compiler_params=_params(1),
        name="out_proj",
    )(merged, xp, xs, wo, g2, w_router_t)


def _router_kernel(lg_ref, u_ref, bl_ref, ui_ref, idx_ref, gate_ref, *, cap):
    n_e, n_r, _ = lg_ref.shape
    lg = lg_ref[...]
    ex = jnp.exp(lg - jnp.max(lg, axis=0, keepdims=True))
    aff = ex / jnp.sum(ex, axis=0, keepdims=True)
    key = lax.bitcast_convert_type(aff, jnp.int32)

    def count(mask):
        return jnp.sum(jnp.sum(mask.astype(F32), axis=2, keepdims=True), axis=1, keepdims=True)

    thr = jnp.zeros((n_e, 1, 1), jnp.int32)
    for bit in range(30, -1, -1):
        cand = thr | jnp.int32(1 << bit)
        thr = jnp.where(count(key >= cand) >= cap, cand, thr)
    gt = key > thr
    eq = key == thr
    need = cap - count(gt)

    def prefix(mask_f32):
        m2 = mask_f32.reshape(n_e * n_r, LANES)
        incl = jnp.dot(m2.astype(BF), u_ref[...], preferred_element_type=F32)
        tot = jnp.broadcast_to(incl[:, LANES - 1:LANES], incl.shape)
        rowoff = jnp.dot(bl_ref[...], tot.astype(BF), preferred_element_type=F32)
        return rowoff + incl - m2, incl

    eq_rank, _ = prefix(eq.astype(F32))
    take_eq = eq & (eq_rank.reshape(n_e, n_r, LANES) < need)
    sel = jnp.where(gt, 1.0, jnp.where(take_eq, 1.0, 0.0))
    _, incl = prefix(sel)
    sel2 = sel.reshape(n_e * n_r, LANES).astype(BF)

    def to_row(col):
        return jnp.transpose(jnp.broadcast_to(col, (cap, LANES)))[0:1, :]

    ones8 = jnp.ones((8, LANES), BF)
    slot = lax.broadcasted_iota(jnp.int32, (cap, 1), 0).astype(F32)
    row_id = lax.broadcasted_iota(jnp.int32, (cap, n_r), 1).astype(F32)
    lane_id = lax.broadcasted_iota(jnp.int32, (cap, LANES), 1).astype(F32)
    nt = (((1,), (1,)), ((), ()))
    for e in range(n_e):
        rows = slice(e * n_r, (e + 1) * n_r)
        tot_row = lax.dot_general(ones8, sel2[rows], nt, preferred_element_type=F32)[0:1]
        cum_row = jnp.dot(tot_row.astype(BF), ui_ref[...], preferred_element_type=F32)
        r_s = jnp.sum((cum_row <= slot).astype(F32), axis=1, keepdims=True)
        onehot = row_id == r_s
        oh = jnp.where(onehot, 1.0, 0.0).astype(BF)
        row_start = jnp.sum(jnp.where(onehot, cum_row - tot_row, 0.0), axis=1, keepdims=True)
        g_incl = jnp.dot(oh, incl[rows].astype(BF), preferred_element_type=F32)
        l_s = jnp.sum((g_incl <= slot - row_start).astype(F32), axis=1, keepdims=True)
        idx_ref[e:e + 1, :] = to_row(r_s * LANES + l_s).astype(jnp.int32)
        a = aff[e]
        a1 = a.astype(BF)
        a2 = (a - a1.astype(F32)).astype(BF)
        a3 = (a - a1.astype(F32) - a2.astype(F32)).astype(BF)
        g_aff = (jnp.dot(oh, a1, preferred_element_type=F32)
                 + jnp.dot(oh, a2, preferred_element_type=F32)) + jnp.dot(oh, a3, preferred_element_type=F32)
        gate_ref[e:e + 1, :] = to_row(jnp.sum(jnp.where(lane_id == l_s, g_aff, 0.0), axis=1, keepdims=True))


def _route(logits, cap):
    n_g, n_e, n_r, _ = logits.shape
    a = np.arange(LANES)
    upper = jnp.asarray(a[:, None] <= a[None, :], BF)
    i = np.arange(n_e * n_r)
    block_lower = jnp.asarray((i[:, None] // n_r == i[None, :] // n_r) & (i[None, :] < i[:, None]), BF)
    r = np.arange(n_r)
    upper_r = jnp.asarray(r[:, None] <= r[None, :], BF)
    return pl.pallas_call(
        functools.partial(_router_kernel, cap=cap),
        grid=(n_g,),
        in_specs=[pl.BlockSpec((None, n_e, n_r, LANES), lambda g: (g, 0, 0, 0)),
                  pl.BlockSpec((LANES, LANES), lambda g: (0, 0)),
                  pl.BlockSpec((n_e * n_r, n_e * n_r), lambda g: (0, 0)),
                  pl.BlockSpec((n_r, n_r), lambda g: (0, 0))],
        out_specs=[pl.BlockSpec((None, n_e, cap), lambda g: (g, 0, 0)),
                   pl.BlockSpec((None, n_e, cap), lambda g: (g, 0, 0))],
        out_shape=[jax.ShapeDtypeStruct((n_g, n_e, cap), jnp.int32),
                   jax.ShapeDtypeStruct((n_g, n_e, cap), F32)],
        compiler_params=_params(1),
        name="router",
    )(logits, upper, block_lower, upper_r)


def _expert_kernel(idx_ref, h2_hbm, gate_ref, wg_ref, wu_ref, wd_ref, xp_in, xs_in,
                   yp_hbm, ys_hbm, xb, xg, stage, acc, sem_x, sem_res, sem_in, sem_out, *,
                   cap, n_group_tok, chunk, n_f, rows_per_step, n_pref, res_rows_per_step):
    del xp_in, xs_in
    e = pl.program_id(0)
    f = pl.program_id(1)
    n_e = pl.num_programs(0)
    n_slots = 2 * cap
    n_chunks = n_slots // chunk
    pack_rows = xg.shape[0] // (n_f * rows_per_step)

    def x_row_copy(expert, s):
        t = idx_ref[expert * n_slots + jnp.minimum(s, n_slots - 1)]
        src = h2_hbm.at[pl.ds(pl.multiple_of(t * pack_rows, pack_rows), pack_rows)]
        dst = xg.at[pl.ds(pl.multiple_of(s * pack_rows, pack_rows), pack_rows)]
        pltpu.make_async_copy(src, dst, sem_x).start()

    def x_rows_wait(n_rows):
        pltpu.make_async_copy(h2_hbm.at[pl.ds(0, n_rows * pack_rows)],
                              xg.at[pl.ds(0, n_rows * pack_rows)], sem_x).wait()

    @pl.when((f == 0) & (e == 0))
    def _():
        def body(i, carry):
            for j in range(ROW_DMA_UNROLL):
                x_row_copy(0, i * ROW_DMA_UNROLL + j)
            return carry
        lax.fori_loop(0, n_slots // ROW_DMA_UNROLL, body, 0)
        x_rows_wait(n_slots)

    @pl.when((f == 0) & (e > 0))
    def _():
        for _ in range(n_f):
            x_rows_wait(rows_per_step)

    @pl.when(f == 0)
    def _():
        for c in range(n_chunks):
            for j in range(pack_rows):
                w = xg[pl.ds(c * chunk * pack_rows + j, chunk, stride=pack_rows), :]
                lo = lax.bitcast_convert_type(w << 16, F32).astype(BF)
                hi = lax.bitcast_convert_type(w & jnp.uint32(0xFFFF0000), F32).astype(BF)
                xb[c * chunk:(c + 1) * chunk, 2 * j * LANES:(2 * j + 1) * LANES] = lo
                xb[c * chunk:(c + 1) * chunk, (2 * j + 1) * LANES:(2 * j + 2) * LANES] = hi
        acc[...] = jnp.zeros_like(acc)

    nxt = jnp.minimum(e + 1, n_e - 1)
    for j in range(rows_per_step):
        x_row_copy(nxt, f * rows_per_step + j)

    def target(c):
        group = (c * chunk) // cap
        return (yp_hbm, ys_hbm)[group], group * n_group_tok

    def res_row_copy(hbm, tok_off, slot, row, sem, to_vmem):
        t = idx_ref[e * n_slots + slot] - tok_off
        src, dst = hbm.at[pl.ds(t, 1)], stage.at[pl.ds(row, 1)]
        if not to_vmem:
            src, dst = dst, src
        pltpu.make_async_copy(src, dst, sem).start()

    def row_copies(c, to_vmem):
        hbm, tok_off = target(c)
        sem = (sem_in if to_vmem else sem_out).at[c % 2]

        def body(i, carry):
            s0 = pl.multiple_of(i * ROW_DMA_UNROLL, ROW_DMA_UNROLL)
            for j in range(ROW_DMA_UNROLL):
                res_row_copy(hbm, tok_off, c * chunk + s0 + j, (c % 2) * chunk + s0 + j, sem, to_vmem)
            return carry
        lax.fori_loop(0, chunk // ROW_DMA_UNROLL, body, 0)

    def wait_rows(c, to_vmem):
        src, dst = target(c)[0].at[pl.ds(0, chunk)], stage.at[pl.ds((c % 2) * chunk, chunk)]
        sem = (sem_in if to_vmem else sem_out).at[c % 2]
        if not to_vmem:
            src, dst = dst, src
        pltpu.make_async_copy(src, dst, sem).wait()

    y0_hbm, y0_off = target(0)
    for j in range(res_rows_per_step):
        s = f * res_rows_per_step + j
        res_row_copy(y0_hbm, y0_off, jnp.minimum(s, n_pref * chunk - 1),
                     jnp.where(s < n_pref * chunk, s, s + (2 - n_pref) * chunk), sem_res, True)

    x = xb[...]
    a = jnp.dot(x, wg_ref[...].astype(BF), preferred_element_type=F32)
    b = jnp.dot(x, wu_ref[...].astype(BF), preferred_element_type=F32)
    hmid = (a * jax.nn.sigmoid(a) * b).astype(BF)
    dm = acc.shape[1]
    for n in range(dm // DOWN_TILE):
        cols = slice(n * DOWN_TILE, (n + 1) * DOWN_TILE)
        acc[:, cols] += jnp.dot(hmid, wd_ref[:, cols].astype(BF), preferred_element_type=F32)

    @pl.when(f == n_f - 1)
    def _():
        for _ in range(n_f):
            pltpu.make_async_copy(y0_hbm.at[pl.ds(0, res_rows_per_step)],
                                  stage.at[pl.ds(0, res_rows_per_step)], sem_res).wait()
        for c in range(n_chunks):
            nxt = c + 1
            if n_pref <= nxt < n_chunks:
                if nxt >= 2:
                    wait_rows(nxt - 2, False)
                row_copies(nxt, True)
            if c >= n_pref:
                wait_rows(c, True)
            rows = slice(c * chunk, (c + 1) * chunk)
            half = pl.ds((c % 2) * chunk, chunk)
            gate_col = jnp.transpose(jnp.broadcast_to(gate_ref[:, rows], (LANES, chunk)))[:, 0:1]
            stage[half, :] = stage[half, :] + acc[rows, :] * gate_col
            row_copies(c, False)
        for c in range(max(n_chunks - 2, 0), n_chunks):
            wait_rows(c, False)

    @pl.when((f == n_f - 1) & (e == n_e - 1))
    def _():
        for _ in range(n_f):
            x_rows_wait(rows_per_step)


def _experts(idx_flat, h2, gate, w_gate, w_up, w_down, x2p, x2s, cap, tf=256):
    n_e, dm, ff = w_gate.shape
    n_slots = 2 * cap
    chunk = min(GATHER_CHUNK, cap)
    n_group_tok = x2p.shape[0]
    n_f = ff // tf
    rows_per_step = -(-n_slots // n_f)
    pack_rows = dm // (2 * LANES)
    n_pref = 2 if 2 * chunk <= cap else 1
    res_rows_per_step = -(-n_pref * chunk // (8 * n_f)) * 8
    spare_rows = n_f * res_rows_per_step - n_pref * chunk
    any_spec = pl.BlockSpec(memory_space=pl.ANY)
    grid_spec = pltpu.PrefetchScalarGridSpec(
        num_scalar_prefetch=1,
        grid=(n_e, n_f),
        in_specs=[any_spec,
                  pl.BlockSpec((None, 1, n_slots), lambda e, f, idx: (e, 0, 0)),
                  pl.BlockSpec((None, dm, tf), lambda e, f, idx: (e, 0, f)),
                  pl.BlockSpec((None, dm, tf), lambda e, f, idx: (e, 0, f)),
                  pl.BlockSpec((None, tf, dm), lambda e, f, idx: (e, f, 0)),
                  any_spec, any_spec],
        out_specs=[any_spec, any_spec],
        scratch_shapes=[pltpu.VMEM((n_slots, dm), BF),
                        pltpu.VMEM((n_f * rows_per_step * pack_rows, LANES), jnp.uint32),
                        pltpu.VMEM((2 * chunk + spare_rows, dm), F32),
                        pltpu.VMEM((n_slots, dm), F32),
                        pltpu.SemaphoreType.DMA(()),
                        pltpu.SemaphoreType.DMA(()),
                        pltpu.SemaphoreType.DMA((2,)),
                        pltpu.SemaphoreType.DMA((2,))],
    )
    return pl.pallas_call(
        functools.partial(_expert_kernel, cap=cap, n_group_tok=n_group_tok, chunk=chunk, n_f=n_f,
                          rows_per_step=rows_per_step, n_pref=n_pref, res_rows_per_step=res_rows_per_step),
        grid_spec=grid_spec,
        out_shape=[jax.ShapeDtypeStruct(x2p.shape, F32), jax.ShapeDtypeStruct(x2s.shape, F32)],
        input_output_aliases={6: 0, 7: 1},
        compiler_params=_params(2, EXPERT_VMEM_LIMIT),
        name="experts",
    )(idx_flat, h2, gate, w_gate, w_up, w_down, x2p, x2s)


def _rope_tables(max_len):
    half = ROPE_DIM // 2
    inv = ROPE_THETA ** (-np.arange(0, ROPE_DIM, 2, dtype=np.float64) / ROPE_DIM)
    ang = np.arange(max_len, dtype=np.float64)[:, None] * inv[None, :]
    cos, sin = np.cos(ang), np.sin(ang)
    cos_t = np.concatenate([cos, cos, np.ones((max_len, HEAD_DIM - 2 * half))], axis=1)
    sin_t = np.concatenate([-sin, sin, np.zeros((max_len, HEAD_DIM - 2 * half))], axis=1)
    return jnp.asarray(cos_t, F32), jnp.asarray(sin_t, F32)


def kernel(x_prompt, x_sample, mem_prompt, mem_sample, norm1_g, w_in, q_norm_a, k_norm_a, sgu_norm_g,
           sgu_w, sgu_b, mem_norm_g, w_mem_kv, q_norm_m, k_norm_m, w_proj_a, w_proj_b, w_proj_m, w_out,
           norm2_g, w_router, w_gate, w_up, w_down):
    assert norm1_g.shape[0] == 1, "one layer"
    bp, sp, dm = x_prompt.shape
    bs, ss, _ = x_sample.shape
    n_prompt, n_sample = bp * sp, bs * ss
    assert n_prompt == n_sample, "both request groups route the same number of tokens"
    n_e = w_router.shape[-1]
    cap = CAPACITY_FACTOR * n_prompt // n_e
    seq_lens = (sp,) * bp + (ss,) * bs
    seq_bounds = tuple(int(v) for v in np.concatenate([[0], np.cumsum(seq_lens)]))

    xp = x_prompt.reshape(n_prompt, dm)
    xs = x_sample.reshape(n_sample, dm)
    mem = jnp.concatenate([mem_prompt, mem_sample], axis=0)
    cos_t, sin_t = _rope_tables(max(seq_lens))
    qk_gain = jnp.concatenate([q_norm_a, k_norm_a], axis=0)

    def layer0(w):
        return w.reshape(w.shape[1:])

    w_in, sgu_w, sgu_b, w_mem_kv, w_proj_a, w_proj_b, w_proj_m, w_out, w_router, w_gate, w_up, w_down = map(
        layer0, (w_in, sgu_w, sgu_b, w_mem_kv, w_proj_a, w_proj_b, w_proj_m, w_out, w_router, w_gate, w_up,
                 w_down))

    h = _norm1(xp, xs, norm1_g)
    qkvs = _inproj_attn(h, w_in, qk_gain, cos_t, sin_t, seq_bounds)
    z_rest = _inproj_rest(h, w_in, 3 * ATTN_WIDTH)

    oas, lses = [], []
    for qkv, (_, d) in zip(qkvs, ATTN_GROUPS):
        o, lse = _band_attention(qkv, d, seq_bounds)
        oas.append(o)
        lses.append(lse)
    ob = _sgu(z_rest, sgu_norm_g, sgu_w, sgu_b.T)
    kv = _mem_kv(mem, mem_norm_g, w_mem_kv, k_norm_m)
    om = _mem_attn(z_rest, kv, q_norm_m, seq_bounds)

    merged = _merge(oas, lses, ob, om, z_rest, _cast_bf16(w_proj_a), _cast_bf16(w_proj_b),
                    _cast_bf16(w_proj_m), dm)
    x2p, x2s, h2, logits_t = _out_proj(merged, xp, xs, _cast_bf16(w_out), norm2_g, w_router.T)

    logits = logits_t.reshape(n_e, 2, n_prompt // LANES, LANES).transpose(1, 0, 2, 3)
    idx, gate = _route(logits, cap)
    tok_off = jnp.arange(2, dtype=jnp.int32)[:, None, None] * n_prompt
    idx_flat = (idx + tok_off).transpose(1, 0, 2).reshape(-1)
    gate = gate.transpose(1, 0, 2).reshape(n_e, 1, 2 * cap)

    yp, ys = _experts(idx_flat, h2, gate, w_gate, w_up, w_down, x2p, x2s, cap)
    return yp.reshape(bp, sp, dm), ys.reshape(bs, ss, dm)
```

```python
import functools
import math

import numpy as np
import jax
import jax.numpy as jnp
from jax import lax
from jax.experimental import pallas as pl
from jax.experimental.pallas import tpu as pltpu

BF = jnp.bfloat16
F32 = jnp.float32

EPS = 1e-6
NEG = -1e30
LANES = 128
SUBLANES = 8
HEAD_DIM = 128
ATTN_GROUPS = ((128, 1), (512, 4), (2048, 16))
HEADS_PER_GROUP = 4
GROUP_WIDTH = HEADS_PER_GROUP * HEAD_DIM
ATTN_WIDTH = len(ATTN_GROUPS) * GROUP_WIDTH
BAND = 64
ROPE_THETA = 500000.0
ROPE_DIM = HEAD_DIM // 4
SGU_CHUNK = 128
SGU_GROUPS = 4
SGU_WIDTH = 1536
SGU_GROUP_WIDTH = SGU_WIDTH // SGU_GROUPS
MEM_HEADS = 4
MEM_HEAD_DIM = 256
MEM_WIDTH = MEM_HEADS * MEM_HEAD_DIM
CAPACITY_FACTOR = 2
ATTN_SUB_BLOCK = 128
ATTN_Q_BLOCK = 512
INPROJ_ROW_SPLIT = 8
ROW_DMA_UNROLL = 8
DOWN_TILE = 256
VMEM_LIMIT = 52 * 1024 * 1024
EXPERT_VMEM_LIMIT = 60 * 1024 * 1024


def _params(n_axes, vmem=VMEM_LIMIT):
    return pltpu.CompilerParams(dimension_semantics=("arbitrary",) * n_axes, vmem_limit_bytes=vmem)


def _rms(x, gain):
    ms = jnp.mean(x * x, axis=-1, keepdims=True)
    return x * lax.rsqrt(ms + EPS) * gain


def _gelu_tanh(x):
    c = math.sqrt(2.0 / math.pi)
    return x * (0.5 * (1.0 + jnp.tanh(c * (x + 0.044715 * (x * x * x)))))


def _cast_kernel(x_ref, o_ref):
    o_ref[...] = x_ref[...].astype(BF)


def _cast_bf16(w, rows=512):
    r, c = w.shape
    rows = min(rows, r)
    return pl.pallas_call(
        _cast_kernel,
        grid=(r // rows,),
        in_specs=[pl.BlockSpec((rows, c), lambda i: (i, 0))],
        out_specs=pl.BlockSpec((rows, c), lambda i: (i, 0)),
        out_shape=jax.ShapeDtypeStruct((r, c), BF),
        compiler_params=_params(1),
        name="cast_bf16",
    )(w)


def _qk_norm_rope(acc, gain, cos_t, sin_t):
    lane = lax.broadcasted_iota(jnp.int32, (1, HEAD_DIM), 1)
    half = ROPE_DIM // 2
    outs = []
    for h in range(HEADS_PER_GROUP):
        y = _rms(acc[:, h * HEAD_DIM:(h + 1) * HEAD_DIM], gain)
        partner = jnp.where(lane < half, pltpu.roll(y, HEAD_DIM - half, 1), pltpu.roll(y, half, 1))
        outs.append(y * cos_t + partner * sin_t)
    return jnp.concatenate(outs, axis=1)


def _store_dilated(val, scr, o_ref, d, row0):
    n_rows = val.shape[0]
    if d == 1:
        o_ref[0, row0:row0 + n_rows, :] = val.astype(BF)
        return
    for h in range(HEADS_PER_GROUP):
        scr[h, row0:row0 + n_rows, :] = val[:, h * HEAD_DIM:(h + 1) * HEAD_DIM]
    for r in range(d):
        o_ref[r, row0 // d:(row0 + n_rows) // d, :] = jnp.concatenate(
            [scr[h, pl.ds(row0 + r, n_rows // d, stride=d), :] for h in range(HEADS_PER_GROUP)],
            axis=1).astype(BF)


def _norm1_kernel(xp_ref, xs_ref, g_ref, h_ref, *, n_prompt_tiles):
    i = pl.program_id(0)
    for x_ref, is_mine in ((xp_ref, i < n_prompt_tiles), (xs_ref, i >= n_prompt_tiles)):
        @pl.when(is_mine)
        def _(x_ref=x_ref):
            h_ref[...] = _rms(x_ref[...], g_ref[...]).astype(BF)


def _norm1(xp, xs, g1, tm=512):
    dm = xp.shape[1]
    m_tok = xp.shape[0] + xs.shape[0]
    npt = xp.shape[0] // tm
    return pl.pallas_call(
        functools.partial(_norm1_kernel, n_prompt_tiles=npt),
        grid=(m_tok // tm,),
        in_specs=[pl.BlockSpec((tm, dm), lambda i: (jnp.minimum(i, npt - 1), 0)),
                  pl.BlockSpec((tm, dm), lambda i: (jnp.maximum(i - npt, 0), 0)),
                  pl.BlockSpec((1, dm), lambda i: (0, 0))],
        out_specs=pl.BlockSpec((tm, dm), lambda i: (i, 0)),
        out_shape=jax.ShapeDtypeStruct((m_tok, dm), BF),
        compiler_params=_params(1),
        name="norm1",
    )(xp, xs, g1)


def _inproj_attn_kernel(h_ref, w_ref, qkg_ref, cos_ref, sin_ref, o0_ref, o1_ref, o2_ref, scr, *, tm):
    n = pl.program_id(1)
    group = n // 3
    which = n % 3
    w = w_ref[...].astype(BF)
    rq = tm // INPROJ_ROW_SPLIT

    def run(o_ref, d, rope):
        gain = qkg_ref[pl.ds(jnp.minimum(which, 1), 1), :]
        for b in range(INPROJ_ROW_SPLIT):
            rows = slice(b * rq, (b + 1) * rq)
            val = jnp.dot(h_ref[rows, :], w, preferred_element_type=F32)
            if rope:
                val = _qk_norm_rope(val, gain, cos_ref[rows, :], sin_ref[rows, :])
            _store_dilated(val, scr, o_ref, d, b * rq)

    for g, (o_ref, (_, d)) in enumerate(zip((o0_ref, o1_ref, o2_ref), ATTN_GROUPS)):
        @pl.when((group == g) & (which < 2))
        def _(o_ref=o_ref, d=d):
            run(o_ref, d, True)

        @pl.when((group == g) & (which == 2))
        def _(o_ref=o_ref, d=d):
            run(o_ref, d, False)


def _inproj_attn(h, w_in, qk_gain, cos_t, sin_t, seq_bounds, tm=1024):
    m_tok, dm = h.shape
    n_steps = 3 * len(ATTN_GROUPS)

    def pos_block(m):
        start = 0
        for bound in seq_bounds[1:-1]:
            start = jnp.where(m * tm >= bound, bound // tm, start)
        return m - start

    def out_spec(g, d):
        return pl.BlockSpec((None, d, tm // d, GROUP_WIDTH),
                            lambda m, n: (jnp.clip(n - 3 * g, 0, 2), 0, m, 0))

    return pl.pallas_call(
        functools.partial(_inproj_attn_kernel, tm=tm),
        grid=(m_tok // tm, n_steps),
        in_specs=[
            pl.BlockSpec((tm, dm), lambda m, n: (m, 0)),
            pl.BlockSpec((dm, GROUP_WIDTH), lambda m, n: (0, (n % 3) * 3 + n // 3)),
            pl.BlockSpec((2, HEAD_DIM), lambda m, n: (0, 0)),
            pl.BlockSpec((tm, HEAD_DIM), lambda m, n: (pos_block(m), 0)),
            pl.BlockSpec((tm, HEAD_DIM), lambda m, n: (pos_block(m), 0)),
        ],
        out_specs=[out_spec(g, d) for g, (_, d) in enumerate(ATTN_GROUPS)],
        out_shape=[jax.ShapeDtypeStruct((3, d, m_tok // d, GROUP_WIDTH), BF) for _, d in ATTN_GROUPS],
        scratch_shapes=[pltpu.VMEM((HEADS_PER_GROUP, tm, HEAD_DIM), F32)],
        compiler_params=_params(2),
        name="inproj_attn",
    )(h, w_in, qk_gain, cos_t, sin_t)


def _matmul_kernel(a_ref, w_ref, o_ref):
    o_ref[...] = jnp.dot(a_ref[...], w_ref[...].astype(BF),
                         preferred_element_type=F32).astype(o_ref.dtype)


def _inproj_rest(h, w_in, col0, tm=2048, tn=512):
    m_tok, dm = h.shape
    n_cols = w_in.shape[1] - col0
    tm = min(tm, m_tok)
    off = col0 // tn
    return pl.pallas_call(
        _matmul_kernel,
        grid=(m_tok // tm, n_cols // tn),
        in_specs=[pl.BlockSpec((tm, dm), lambda m, n: (m, 0)),
                  pl.BlockSpec((dm, tn), lambda m, n: (0, off + n))],
        out_specs=pl.BlockSpec((tm, tn), lambda m, n: (m, n)),
        out_shape=jax.ShapeDtypeStruct((m_tok, n_cols), BF),
        compiler_params=_params(2),
        name="inproj_rest",
    )(h, w_in)


def _attn_kernel(q_ref, kp_ref, kc_ref, kn_ref, vp_ref, vc_ref, vn_ref, o_ref, lse_ref, *, bounds, tq):
    sub = ATTN_SUB_BLOCK
    j0 = pl.program_id(1) * tq
    lo = jnp.int32(bounds[0])
    hi = jnp.int32(bounds[1])
    for b0, b1 in zip(bounds[1:-1], bounds[2:]):
        lo = jnp.where(j0 >= b0, b0, lo)
        hi = jnp.where(j0 >= b0, b1, hi)
    k_all = jnp.concatenate([kp_ref[...], kc_ref[...], kn_ref[...]], axis=0)
    v_all = jnp.concatenate([vp_ref[...], vc_ref[...], vn_ref[...]], axis=0)
    scale = 1.0 / math.sqrt(HEAD_DIM)
    lane = lax.broadcasted_iota(jnp.int32, (1, LANES), 1)
    n_keys = sub + 2 * BAND
    chains = [(u, h) for u in range(tq // sub) for h in range(HEADS_PER_GROUP)]

    def head(h):
        return slice(h * HEAD_DIM, (h + 1) * HEAD_DIM)

    def keys(u):
        return slice((u + 1) * sub - BAND, (u + 1) * sub - BAND + n_keys)

    valid = []
    for u in range(tq // sub):
        qpos = j0 + u * sub + lax.broadcasted_iota(jnp.int32, (sub, 1), 0)
        kpos = j0 + u * sub - BAND + lax.broadcasted_iota(jnp.int32, (1, n_keys), 1)
        valid.append((jnp.abs(kpos - qpos) <= BAND) & (kpos >= lo) & (kpos < hi))
    s = [lax.dot_general(q_ref[u * sub:(u + 1) * sub, head(h)], k_all[keys(u), head(h)],
                         (((1,), (1,)), ((), ())), preferred_element_type=F32) for u, h in chains]
    s = [jnp.where(valid[u], si * scale, NEG) for (u, h), si in zip(chains, s)]
    mx = [jnp.max(si, axis=-1, keepdims=True) for si in s]
    p = [jnp.exp(si - mi) for si, mi in zip(s, mx)]
    den = [jnp.sum(pi, axis=-1, keepdims=True) for pi in p]
    o = [jnp.dot(pi.astype(BF), v_all[keys(u), head(h)], preferred_element_type=F32)
         for (u, h), pi in zip(chains, p)]
    for u in range(tq // sub):
        rows = slice(u * sub, (u + 1) * sub)
        lse_all = jnp.zeros((sub, LANES), F32)
        for h in range(HEADS_PER_GROUP):
            c = chains.index((u, h))
            o_ref[rows, head(h)] = (o[c] / den[c]).astype(BF)
            lse_all = jnp.where(lane == h, mx[c] + jnp.log(den[c]), lse_all)
        lse_ref[rows, :] = lse_all


def _band_attention(qkv, d, seq_bounds):
    _, _, n_j, _ = qkv.shape
    sub = ATTN_SUB_BLOCK
    bounds = tuple(b // d for b in seq_bounds)
    tq = max(t for t in (sub, 2 * sub, ATTN_Q_BLOCK) if all(b % t == 0 for b in bounds))
    n_blocks = n_j // tq
    n_sub = n_j // sub
    per = tq // sub

    def spec(which, shift):
        if shift == 0:
            return pl.BlockSpec((None, None, tq, GROUP_WIDTH), lambda r, i: (which, r, i, 0))
        halo = -1 if shift < 0 else per
        return pl.BlockSpec((None, None, sub, GROUP_WIDTH),
                            lambda r, i: (which, r, jnp.clip(i * per + halo, 0, n_sub - 1), 0))

    return pl.pallas_call(
        functools.partial(_attn_kernel, bounds=bounds, tq=tq),
        grid=(d, n_blocks),
        in_specs=[spec(0, 0), spec(1, -1), spec(1, 0), spec(1, 1), spec(2, -1), spec(2, 0), spec(2, 1)],
        out_specs=[pl.BlockSpec((None, tq, GROUP_WIDTH), lambda r, i: (r, i, 0)),
                   pl.BlockSpec((None, tq, LANES), lambda r, i: (r, i, 0))],
        out_shape=[jax.ShapeDtypeStruct((d, n_j, GROUP_WIDTH), BF),
                   jax.ShapeDtypeStruct((d, n_j, LANES), F32)],
        compiler_params=_params(2),
        name=f"band_attention_d{d}",
    )(qkv, qkv, qkv, qkv, qkv, qkv, qkv)


def _sgu_kernel(u_ref, v_ref, ng_ref, ws_ref, bs_ref, o_ref, *, tb):
    vn = _rms(_gelu_tanh(v_ref[...].astype(F32)), ng_ref[...]).astype(BF)
    for g in range(SGU_GROUPS):
        cols = slice(g * SGU_GROUP_WIDTH, (g + 1) * SGU_GROUP_WIDTH)
        w = ws_ref[g].astype(BF)
        bias = bs_ref[:, g:g + 1]
        for c in range(tb // SGU_CHUNK):
            rows = slice(c * SGU_CHUNK, (c + 1) * SGU_CHUNK)
            mixed = jnp.dot(w, vn[rows, cols], preferred_element_type=F32) + bias
            u = _gelu_tanh(u_ref[rows, cols].astype(F32))
            o_ref[rows, cols] = (u * mixed).astype(BF)


def _sgu(z_rest, norm_g, w_s, b_s_t, tb=512):
    m_tok = z_rest.shape[0]
    return pl.pallas_call(
        functools.partial(_sgu_kernel, tb=tb),
        grid=(m_tok // tb,),
        in_specs=[pl.BlockSpec((tb, SGU_WIDTH), lambda i: (i, 0)),
                  pl.BlockSpec((tb, SGU_WIDTH), lambda i: (i, 1)),
                  pl.BlockSpec((1, SGU_WIDTH), lambda i: (0, 0)),
                  pl.BlockSpec((SGU_GROUPS, SGU_CHUNK, SGU_CHUNK), lambda i: (0, 0, 0)),
                  pl.BlockSpec((SGU_CHUNK, SGU_GROUPS), lambda i: (0, 0))],
        out_specs=pl.BlockSpec((tb, SGU_WIDTH), lambda i: (i, 0)),
        out_shape=jax.ShapeDtypeStruct((m_tok, SGU_WIDTH), BF),
        compiler_params=_params(1),
        name="sgu",
    )(z_rest, z_rest, norm_g, w_s, b_s_t)


def _mem_kv_kernel(mem_ref, ng_ref, w_ref, kg_ref, o_ref, *, n_k_tiles, tn):
    n = pl.program_id(1)
    hm = _rms(mem_ref[...], ng_ref[...]).astype(BF)
    kv = jnp.dot(hm, w_ref[...].astype(BF), preferred_element_type=F32)

    @pl.when(n < n_k_tiles)
    def _():
        for h in range(tn // MEM_HEAD_DIM):
            cols = slice(h * MEM_HEAD_DIM, (h + 1) * MEM_HEAD_DIM)
            o_ref[:, cols] = _rms(kv[:, cols], kg_ref[...]).astype(BF)

    @pl.when(n >= n_k_tiles)
    def _():
        o_ref[...] = kv.astype(BF)


def _mem_kv(mem, norm_g, w_kv, k_gain, tn=512):
    n_b, n_mem, dm = mem.shape
    width = w_kv.shape[1]
    return pl.pallas_call(
        functools.partial(_mem_kv_kernel, n_k_tiles=MEM_WIDTH // tn, tn=tn),
        grid=(n_b, width // tn),
        in_specs=[pl.BlockSpec((None, n_mem, dm), lambda b, n: (b, 0, 0)),
                  pl.BlockSpec((1, dm), lambda b, n: (0, 0)),
                  pl.BlockSpec((dm, tn), lambda b, n: (0, n)),
                  pl.BlockSpec((1, MEM_HEAD_DIM), lambda b, n: (0, 0))],
        out_specs=pl.BlockSpec((None, n_mem, tn), lambda b, n: (b, 0, n)),
        out_shape=jax.ShapeDtypeStruct((n_b, n_mem, width), BF),
        compiler_params=_params(2),
        name="mem_kv",
    )(mem, norm_g, w_kv, k_gain)


def _mem_attn_kernel(q_ref, kv_ref, qg_ref, o_ref):
    scale = 1.0 / math.sqrt(MEM_HEAD_DIM)
    heads = [slice(h * MEM_HEAD_DIM, (h + 1) * MEM_HEAD_DIM) for h in range(MEM_HEADS)]
    nt = (((1,), (1,)), ((), ()))
    q = [_rms(q_ref[:, c].astype(F32), qg_ref[...]).astype(BF) for c in heads]
    s = [lax.dot_general(qh, kv_ref[:, c], nt, preferred_element_type=F32) * scale for qh, c in zip(q, heads)]
    mx = [jnp.max(sh, axis=-1, keepdims=True) for sh in s]
    p = [jnp.exp(sh - mh) for sh, mh in zip(s, mx)]
    den = [jnp.sum(ph, axis=-1, keepdims=True) for ph in p]
    o = [jnp.dot(ph.astype(BF), kv_ref[:, MEM_WIDTH + c.start:MEM_WIDTH + c.stop], preferred_element_type=F32)
         for ph, c in zip(p, heads)]
    for c, oh, dh in zip(heads, o, den):
        o_ref[:, c] = (oh / dh).astype(BF)


def _mem_attn(z_rest, kv, q_gain, batch_bounds, tm=512):
    m_tok = z_rest.shape[0]
    n_mem = kv.shape[1]
    q_block = (2 * SGU_WIDTH) // MEM_WIDTH

    def batch_of(i):
        b = 0
        for bound in batch_bounds[1:-1]:
            b = b + (i * tm >= bound).astype(jnp.int32)
        return b

    return pl.pallas_call(
        _mem_attn_kernel,
        grid=(m_tok // tm,),
        in_specs=[pl.BlockSpec((tm, MEM_WIDTH), lambda i: (i, q_block)),
                  pl.BlockSpec((None, n_mem, 2 * MEM_WIDTH), lambda i: (batch_of(i), 0, 0)),
                  pl.BlockSpec((1, MEM_HEAD_DIM), lambda i: (0, 0))],
        out_specs=pl.BlockSpec((tm, MEM_WIDTH), lambda i: (i, 0)),
        out_shape=jax.ShapeDtypeStruct((m_tok, MEM_WIDTH), BF),
        compiler_params=_params(1),
        name="mem_attn",
    )(z_rest, kv, q_gain)


def _merge_kernel(oa0_ref, oa1_ref, oa2_ref, l0_ref, l1_ref, l2_ref, ob_ref, om_ref,
                  g0_ref, g1_ref, g2_ref, wa_ref, wb_ref, wm_ref, o_ref, oa_scr, lse_scr, *, tm):
    n_h = HEADS_PER_GROUP
    n_g = len(ATTN_GROUPS)
    for g, (oa_ref, l_ref, (_, d)) in enumerate(zip((oa0_ref, oa1_ref, oa2_ref),
                                                    (l0_ref, l1_ref, l2_ref), ATTN_GROUPS)):
        for r in range(d):
            rows = pl.ds(r, tm // d, stride=d) if d > 1 else slice(None)
            o = oa_ref[r].astype(F32)
            for h in range(n_h):
                oa_scr[g * n_h + h, rows, :] = o[:, h * HEAD_DIM:(h + 1) * HEAD_DIM]
            lse_scr[g, rows, :] = l_ref[r]
    lses = [lse_scr[g] for g in range(n_g)]
    mx = jnp.maximum(jnp.maximum(lses[0], lses[1]), lses[2])
    es = [jnp.exp(l - mx) for l in lses]
    den = es[0] + es[1] + es[2]
    slabs = []
    for g in range(n_g):
        alpha = es[g] / den
        slabs += [(oa_scr[g * n_h + h] * alpha[:, h:h + 1]).astype(BF) for h in range(n_h)]
    pa = jnp.dot(jnp.concatenate(slabs, axis=1), wa_ref[...], preferred_element_type=F32)
    merged = jax.nn.sigmoid(g0_ref[...].astype(F32)) * pa
    pb = jnp.dot(ob_ref[...], wb_ref[...], preferred_element_type=F32)
    merged = merged + jax.nn.sigmoid(g1_ref[...].astype(F32)) * pb
    pm = jnp.dot(om_ref[...], wm_ref[...], preferred_element_type=F32)
    merged = merged + jax.nn.sigmoid(g2_ref[...].astype(F32)) * pm
    o_ref[...] = merged.astype(BF)


def _resident(shape):
    return pl.BlockSpec(shape, lambda i: (0,) * len(shape), pipeline_mode=pl.Buffered(1))


def _merge(oas, lses, ob, om, z_rest, wa, wb, wm, dm, tm=512):
    m_tok = ob.shape[0]
    gate0 = (2 * SGU_WIDTH + MEM_WIDTH) // dm
    oa_specs = [pl.BlockSpec((d, tm // d, GROUP_WIDTH), lambda i: (0, i, 0)) for _, d in ATTN_GROUPS]
    l_specs = [pl.BlockSpec((d, tm // d, LANES), lambda i: (0, i, 0)) for _, d in ATTN_GROUPS]
    g_specs = [pl.BlockSpec((tm, dm), lambda i, b=b: (i, gate0 + b)) for b in range(3)]
    return pl.pallas_call(
        functools.partial(_merge_kernel, tm=tm),
        grid=(m_tok // tm,),
        in_specs=oa_specs + l_specs
        + [pl.BlockSpec((tm, SGU_WIDTH), lambda i: (i, 0)), pl.BlockSpec((tm, MEM_WIDTH), lambda i: (i, 0))]
        + g_specs + [_resident(wa.shape), _resident(wb.shape), _resident(wm.shape)],
        out_specs=pl.BlockSpec((tm, dm), lambda i: (i, 0)),
        out_shape=jax.ShapeDtypeStruct((m_tok, dm), BF),
        scratch_shapes=[pltpu.VMEM((len(ATTN_GROUPS) * HEADS_PER_GROUP, tm, HEAD_DIM), F32),
                        pltpu.VMEM((len(ATTN_GROUPS), tm, LANES), F32)],
        compiler_params=_params(1),
        name="merge",
    )(*oas, *lses, ob, om, z_rest, z_rest, z_rest, wa, wb, wm)


def _out_kernel(mg_ref, xp_in, xs_in, wo_ref, g2_ref, wr_ref, xp_ref, xs_ref, h2_ref, lg_ref, *,
                n_prompt_tiles):
    i = pl.program_id(0)
    tm = mg_ref.shape[0]
    halves = [slice(u * (tm // 2), (u + 1) * (tm // 2)) for u in range(2)]
    nt = (((1,), (1,)), ((), ()))

    def run(x_in, x_out):
        wr = wr_ref[...]
        w_hi = wr.astype(BF)
        w_lo = (wr - w_hi.astype(F32)).astype(BF)
        proj = [jnp.dot(mg_ref[r, :], wo_ref[...], preferred_element_type=F32) for r in halves]
        x2 = [x_in[r, :] + p for r, p in zip(halves, proj)]
        for r, v in zip(halves, x2):
            x_out[r, :] = v
        h2 = [_rms(v, g2_ref[...]) for v in x2]
        for r, v in zip(halves, h2):
            for k in range(v.shape[1] // (SUBLANES * LANES)):
                for j in range(SUBLANES):
                    c = (k * SUBLANES + j) * LANES
                    h2_ref[k, pl.ds(r.start * SUBLANES + j, r.stop - r.start, stride=SUBLANES), :] = (
                        v[:, c:c + LANES])
        for r, v in zip(halves, h2):
            h_hi = v.astype(BF)
            h_lo = (v - h_hi.astype(F32)).astype(BF)
            lg = lax.dot_general(w_hi, h_hi, nt, preferred_element_type=F32)
            lg = lg + lax.dot_general(w_hi, h_lo, nt, preferred_element_type=F32)
            lg = lg + lax.dot_general(w_lo, h_hi, nt, preferred_element_type=F32)
            lg_ref[:, r] = lg

    @pl.when(i < n_prompt_tiles)
    def _():
        run(xp_in, xp_ref)

    @pl.when(i >= n_prompt_tiles)
    def _():
        run(xs_in, xs_ref)


def _out_proj(merged, xp, xs, wo, g2, w_router_t, tm=256):
    n_prompt, dm = xp.shape
    m_tok = n_prompt + xs.shape[0]
    n_e = w_router_t.shape[0]
    npt = n_prompt // tm
    n_tiles = dm // (SUBLANES * LANES)
    return pl.pallas_call(
        functools.partial(_out_kernel, n_prompt_tiles=npt),
        grid=(m_tok // tm,),
        in_specs=[pl.BlockSpec((tm, dm), lambda i: (i, 0)),
                  pl.BlockSpec((tm, dm), lambda i: (jnp.minimum(i, npt - 1), 0)),
                  pl.BlockSpec((tm, dm), lambda i: (jnp.maximum(i - npt, 0), 0)),
                  _resident(wo.shape),
                  pl.BlockSpec((1, dm), lambda i: (0, 0)),
                  pl.BlockSpec((n_e, dm), lambda i: (0, 0))],
        out_specs=[pl.BlockSpec((tm, dm), lambda i: (jnp.minimum(i, npt - 1), 0)),
                   pl.BlockSpec((tm, dm), lambda i: (jnp.maximum(i - npt, 0), 0)),
                   pl.BlockSpec((n_tiles, tm * SUBLANES, LANES), lambda i: (0, i, 0)),
                   pl.BlockSpec((n_e, tm), lambda i: (0, i))],
        out_shape=[jax.ShapeDtypeStruct((n_prompt, dm), F32),
                   jax.ShapeDtypeStruct((m_tok - n_prompt, dm), F32),
                   jax.ShapeDtypeStruct((n_tiles, m_tok * SUBLANES, LANES), F32),
                   jax.ShapeDtypeStruct((n_e, m_tok), F32)],
        compiler_params=_params(1),
        name="out_proj",
    )(merged, xp, xs, wo, g2, w_router_t)


def _router_kernel(lg_ref, u_ref, bl_ref, ui_ref, idx_ref, gate_ref, *, cap):
    n_e, n_r, _ = lg_ref.shape
    lg = lg_ref[...]
    ex = jnp.exp(lg - jnp.max(lg, axis=0, keepdims=True))
    aff = ex / jnp.sum(ex, axis=0, keepdims=True)

    def count(mask):
        return jnp.sum(jnp.sum(mask.astype(F32), axis=2, keepdims=True), axis=1, keepdims=True)

    thr_bits = jnp.zeros((n_e, 1, 1), jnp.int32)
    for bit in range(30, -1, -1):
        cand = thr_bits | jnp.int32(1 << bit)
        enough = count(aff >= lax.bitcast_convert_type(cand, F32)) >= cap
        thr_bits = jnp.where(enough, cand, thr_bits)
    thr = lax.bitcast_convert_type(thr_bits, F32)
    gt = aff > thr
    eq = aff == thr
    need = cap - count(gt)

    def prefix(mask_f32):
        m2 = mask_f32.reshape(n_e * n_r, LANES)
        incl = jnp.dot(m2.astype(BF), u_ref[...], preferred_element_type=F32)
        tot = jnp.broadcast_to(incl[:, LANES - 1:LANES], incl.shape)
        rowoff = jnp.dot(bl_ref[...], tot.astype(BF), preferred_element_type=F32)
        return rowoff + incl - m2, incl

    eq_rank, _ = prefix(eq.astype(F32))
    take_eq = eq & (eq_rank.reshape(n_e, n_r, LANES) < need)
    sel = jnp.where(gt, 1.0, jnp.where(take_eq, 1.0, 0.0))
    _, incl = prefix(sel)
    sel2 = sel.reshape(n_e * n_r, LANES).astype(BF)

    def to_row(col):
        return jnp.transpose(jnp.broadcast_to(col, (cap, LANES)))[0:1, :]

    ones8 = jnp.ones((8, LANES), BF)
    slot = lax.broadcasted_iota(jnp.int32, (cap, 1), 0).astype(F32)
    row_id = lax.broadcasted_iota(jnp.int32, (cap, n_r), 1).astype(F32)
    lane_id = lax.broadcasted_iota(jnp.int32, (cap, LANES), 1).astype(F32)
    nt = (((1,), (1,)), ((), ()))
    for e in range(n_e):
        rows = slice(e * n_r, (e + 1) * n_r)
        tot_row = lax.dot_general(ones8, sel2[rows], nt, preferred_element_type=F32)[0:1]
        cum_row = jnp.dot(tot_row.astype(BF), ui_ref[...], preferred_element_type=F32)
        r_s = jnp.sum((cum_row <= slot).astype(F32), axis=1, keepdims=True)
        onehot = row_id == r_s
        oh = jnp.where(onehot, 1.0, 0.0).astype(BF)
        row_start = jnp.sum(jnp.where(onehot, cum_row - tot_row, 0.0), axis=1, keepdims=True)
        g_incl = jnp.dot(oh, incl[rows].astype(BF), preferred_element_type=F32)
        l_s = jnp.sum((g_incl <= slot - row_start).astype(F32), axis=1, keepdims=True)
        idx_ref[e:e + 1, :] = to_row(r_s * LANES + l_s).astype(jnp.int32)
        a = aff[e]
        a1 = a.astype(BF)
        a2 = (a - a1.astype(F32)).astype(BF)
        a3 = (a - a1.astype(F32) - a2.astype(F32)).astype(BF)
        g_aff = (jnp.dot(oh, a1, preferred_element_type=F32)
                 + jnp.dot(oh, a2, preferred_element_type=F32)) + jnp.dot(oh, a3, preferred_element_type=F32)
        gate_ref[e:e + 1, :] = to_row(jnp.sum(jnp.where(lane_id == l_s, g_aff, 0.0), axis=1, keepdims=True))


def _route(logits, cap):
    n_g, n_e, n_r, _ = logits.shape
    a = np.arange(LANES)
    upper = jnp.asarray(a[:, None] <= a[None, :], BF)
    i = np.arange(n_e * n_r)
    block_lower = jnp.asarray((i[:, None] // n_r == i[None, :] // n_r) & (i[None, :] < i[:, None]), BF)
    r = np.arange(n_r)
    upper_r = jnp.asarray(r[:, None] <= r[None, :], BF)
    return pl.pallas_call(
        functools.partial(_router_kernel, cap=cap),
        grid=(n_g,),
        in_specs=[pl.BlockSpec((None, n_e, n_r, LANES), lambda g: (g, 0, 0, 0)),
                  pl.BlockSpec((LANES, LANES), lambda g: (0, 0)),
                  pl.BlockSpec((n_e * n_r, n_e * n_r), lambda g: (0, 0)),
                  pl.BlockSpec((n_r, n_r), lambda g: (0, 0))],
        out_specs=[pl.BlockSpec((None, n_e, cap), lambda g: (g, 0, 0)),
                   pl.BlockSpec((None, n_e, cap), lambda g: (g, 0, 0))],
        out_shape=[jax.ShapeDtypeStruct((n_g, n_e, cap), jnp.int32),
                   jax.ShapeDtypeStruct((n_g, n_e, cap), F32)],
        compiler_params=_params(1),
        name="router",
    )(logits, upper, block_lower, upper_r)


def _expert_kernel(idx_ref, h2_hbm, gate_ref, wg_ref, wu_ref, wd_ref, xp_in, xs_in,
                   yp_hbm, ys_hbm, xb, xg, stage, acc, sem_x, sem_in, sem_out, *,
                   cap, n_group_tok, n_f, rows_per_step, res_rows_per_step):
    del xp_in, xs_in
    e = pl.program_id(0)
    g = pl.program_id(1)
    f = pl.program_id(2)
    unit = 2 * e + g
    n_units = 2 * pl.num_programs(0)
    n_tiles = xg.shape[0]
    n_in_step = n_f * res_rows_per_step
    y_hbm = (yp_hbm, ys_hbm)

    def x_row_copy(u, s):
        t = idx_ref[u * cap + jnp.minimum(s, cap - 1)]
        for k in range(n_tiles):
            src = h2_hbm.at[k, pl.ds(pl.multiple_of(t * SUBLANES, SUBLANES), SUBLANES)]
            dst = xg.at[k, pl.ds(pl.multiple_of(s * SUBLANES, SUBLANES), SUBLANES)]
            pltpu.make_async_copy(src, dst, sem_x).start()

    def x_rows_wait(n_rows):
        pltpu.make_async_copy(h2_hbm.at[:, pl.ds(0, n_rows * SUBLANES)],
                              xg.at[:, pl.ds(0, n_rows * SUBLANES)], sem_x).wait()

    def res_row_copy(grp, u, slot, to_vmem):
        t = idx_ref[u * cap + slot] - grp * n_group_tok
        src, dst = y_hbm[grp].at[pl.ds(t, 1)], stage.at[grp, pl.ds(slot, 1)]
        if not to_vmem:
            src, dst = dst, src
        pltpu.make_async_copy(src, dst, (sem_in if to_vmem else sem_out).at[grp]).start()

    def res_rows_loop(grp, u, lo, hi, to_vmem):
        def body(i, carry):
            s0 = pl.multiple_of(lo + i * ROW_DMA_UNROLL, ROW_DMA_UNROLL)
            for j in range(ROW_DMA_UNROLL):
                res_row_copy(grp, u, s0 + j, to_vmem)
            return carry
        lax.fori_loop(0, (hi - lo) // ROW_DMA_UNROLL, body, 0)

    def res_rows_wait(grp, to_vmem):
        src, dst = y_hbm[grp].at[pl.ds(0, cap)], stage.at[grp]
        if not to_vmem:
            src, dst = dst, src
        pltpu.make_async_copy(src, dst, (sem_in if to_vmem else sem_out).at[grp]).wait()

    @pl.when((f == 0) & (unit == 0))
    def _():
        def body(i, carry):
            for j in range(ROW_DMA_UNROLL):
                x_row_copy(0, i * ROW_DMA_UNROLL + j)
            return carry
        lax.fori_loop(0, cap // ROW_DMA_UNROLL, body, 0)
        x_rows_wait(cap)

    @pl.when((f == 0) & (unit > 0))
    def _():
        for _ in range(n_f):
            x_rows_wait(rows_per_step)

    for grp in range(2):
        @pl.when((f == 0) & (g == grp))
        def _(grp=grp):
            for k in range(n_tiles):
                for j in range(SUBLANES):
                    c = (k * SUBLANES + j) * LANES
                    xb[:, c:c + LANES] = xg[k, pl.ds(j, cap, stride=SUBLANES), :].astype(BF)
            acc[...] = jnp.zeros_like(acc)
            res_rows_loop(grp, unit, n_in_step, cap, True)

    def main_block(grp, scatter_prev):
        nxt = jnp.minimum(unit + 1, n_units - 1)
        for j in range(rows_per_step):
            x_row_copy(nxt, f * rows_per_step + j)
        for j in range(res_rows_per_step):
            s = f * res_rows_per_step + j
            res_row_copy(grp, unit, s, True)
            if scatter_prev:
                res_row_copy(1 - grp, unit - 1, s, False)
        x = xb[...]
        a = jnp.dot(x, wg_ref[...].astype(BF), preferred_element_type=F32)
        b = jnp.dot(x, wu_ref[...].astype(BF), preferred_element_type=F32)
        hmid = (a * jax.nn.sigmoid(a) * b).astype(BF)
        dm = acc.shape[1]
        for n in range(dm // DOWN_TILE):
            cols = slice(n * DOWN_TILE, (n + 1) * DOWN_TILE)
            acc[:, cols] += jnp.dot(hmid, wd_ref[:, cols].astype(BF), preferred_element_type=F32)

    @pl.when(unit == 0)
    def _():
        main_block(0, False)

    @pl.when((g == 0) & (e > 0))
    def _():
        main_block(0, True)

    @pl.when(g == 1)
    def _():
        main_block(1, True)

    for grp in range(2):
        @pl.when((f == n_f - 1) & (g == grp))
        def _(grp=grp):
            res_rows_wait(grp, True)
            half = cap // 2
            for r in range(2):
                rows = slice(r * half, (r + 1) * half)
                gate_col = jnp.transpose(jnp.broadcast_to(gate_ref[:, rows], (LANES, half)))[:, 0:1]
                stage[grp, rows, :] = stage[grp, rows, :] + acc[rows, :] * gate_col
            res_rows_loop(grp, unit, n_in_step, cap, False)

        @pl.when((f == n_f - 1) & (g == grp) & (unit > 0))
        def _(grp=grp):
            res_rows_wait(1 - grp, False)

    @pl.when((f == n_f - 1) & (unit == n_units - 1))
    def _():
        res_rows_loop(1, unit, 0, n_in_step, False)
        res_rows_wait(1, False)
        for _ in range(n_f):
            x_rows_wait(rows_per_step)


def _experts(idx_flat, h2, gate, w_gate, w_up, w_down, x2p, x2s, cap, tf=256):
    n_e, dm, ff = w_gate.shape
    n_group_tok = x2p.shape[0]
    n_f = ff // tf
    rows_per_step = -(-cap // n_f)
    n_tiles = dm // (SUBLANES * LANES)
    unit_rows = ROW_DMA_UNROLL // math.gcd(n_f, ROW_DMA_UNROLL)
    res_rows_per_step = (cap // n_f) // unit_rows * unit_rows
    any_spec = pl.BlockSpec(memory_space=pl.ANY)
    grid_spec = pltpu.PrefetchScalarGridSpec(
        num_scalar_prefetch=1,
        grid=(n_e, 2, n_f),
        in_specs=[any_spec,
                  pl.BlockSpec((None, 1, cap), lambda e, g, f, idx: (e, 0, g)),
                  pl.BlockSpec((None, dm, tf), lambda e, g, f, idx: (e, 0, f)),
                  pl.BlockSpec((None, dm, tf), lambda e, g, f, idx: (e, 0, f)),
                  pl.BlockSpec((None, tf, dm), lambda e, g, f, idx: (e, f, 0)),
                  any_spec, any_spec],
        out_specs=[any_spec, any_spec],
        scratch_shapes=[pltpu.VMEM((cap, dm), BF),
                        pltpu.VMEM((n_tiles, n_f * rows_per_step * SUBLANES, LANES), F32),
                        pltpu.VMEM((2, cap, dm), F32),
                        pltpu.VMEM((cap, dm), F32),
                        pltpu.SemaphoreType.DMA(()),
                        pltpu.SemaphoreType.DMA((2,)),
                        pltpu.SemaphoreType.DMA((2,))],
    )
    return pl.pallas_call(
        functools.partial(_expert_kernel, cap=cap, n_group_tok=n_group_tok, n_f=n_f,
                          rows_per_step=rows_per_step, res_rows_per_step=res_rows_per_step),
        grid_spec=grid_spec,
        out_shape=[jax.ShapeDtypeStruct(x2p.shape, F32), jax.ShapeDtypeStruct(x2s.shape, F32)],
        input_output_aliases={6: 0, 7: 1},
        compiler_params=_params(3, EXPERT_VMEM_LIMIT),
        name="experts",
    )(idx_flat, h2, gate, w_gate, w_up, w_down, x2p, x2s)


def _rope_tables(max_len):
    half = ROPE_DIM // 2
    inv = ROPE_THETA ** (-np.arange(0, ROPE_DIM, 2, dtype=np.float64) / ROPE_DIM)
    ang = np.arange(max_len, dtype=np.float64)[:, None] * inv[None, :]
    cos, sin = np.cos(ang), np.sin(ang)
    cos_t = np.concatenate([cos, cos, np.ones((max_len, HEAD_DIM - 2 * half))], axis=1)
    sin_t = np.concatenate([-sin, sin, np.zeros((max_len, HEAD_DIM - 2 * half))], axis=1)
    return jnp.asarray(cos_t, F32), jnp.asarray(sin_t, F32)


def kernel(x_prompt, x_sample, mem_prompt, mem_sample, norm1_g, w_in, q_norm_a, k_norm_a, sgu_norm_g,
           sgu_w, sgu_b, mem_norm_g, w_mem_kv, q_norm_m, k_norm_m, w_proj_a, w_proj_b, w_proj_m, w_out,
           norm2_g, w_router, w_gate, w_up, w_down):
    assert norm1_g.shape[0] == 1, "one layer"
    bp, sp, dm = x_prompt.shape
    bs, ss, _ = x_sample.shape
    n_prompt, n_sample = bp * sp, bs * ss
    assert n_prompt == n_sample, "both request groups route the same number of tokens"
    n_e = w_router.shape[-1]
    cap = CAPACITY_FACTOR * n_prompt // n_e
    seq_lens = (sp,) * bp + (ss,) * bs
    seq_bounds = tuple(int(v) for v in np.concatenate([[0], np.cumsum(seq_lens)]))

    xp = x_prompt.reshape(n_prompt, dm)
    xs = x_sample.reshape(n_sample, dm)
    mem = jnp.concatenate([mem_prompt, mem_sample], axis=0)
    cos_t, sin_t = _rope_tables(max(seq_lens))
    qk_gain = jnp.concatenate([q_norm_a, k_norm_a], axis=0)

    def layer0(w):
        return w.reshape(w.shape[1:])

    w_in, sgu_w, sgu_b, w_mem_kv, w_proj_a, w_proj_b, w_proj_m, w_out, w_router, w_gate, w_up, w_down = map(
        layer0, (w_in, sgu_w, sgu_b, w_mem_kv, w_proj_a, w_proj_b, w_proj_m, w_out, w_router, w_gate, w_up,
                 w_down))

    h = _norm1(xp, xs, norm1_g)
    qkvs = _inproj_attn(h, w_in, qk_gain, cos_t, sin_t, seq_bounds)
    z_rest = _inproj_rest(h, w_in, 3 * ATTN_WIDTH)

    oas, lses = [], []
    for qkv, (_, d) in zip(qkvs, ATTN_GROUPS):
        o, lse = _band_attention(qkv, d, seq_bounds)
        oas.append(o)
        lses.append(lse)
    ob = _sgu(z_rest, sgu_norm_g, sgu_w, sgu_b.T)
    kv = _mem_kv(mem, mem_norm_g, w_mem_kv, k_norm_m)
    om = _mem_attn(z_rest, kv, q_norm_m, seq_bounds)

    merged = _merge(oas, lses, ob, om, z_rest, _cast_bf16(w_proj_a), _cast_bf16(w_proj_b),
                    _cast_bf16(w_proj_m), dm)
    x2p, x2s, h2, logits_t = _out_proj(merged, xp, xs, _cast_bf16(w_out), norm2_g, w_router.T)

    logits = logits_t.reshape(n_e, 2, n_prompt // LANES, LANES).transpose(1, 0, 2, 3)
    idx, gate = _route(logits, cap)
    tok_off = jnp.arange(2, dtype=jnp.int32)[:, None, None] * n_prompt
    idx_flat = (idx + tok_off).transpose(1, 0, 2).reshape(-1)
    gate = gate.transpose(1, 0, 2).reshape(n_e, 1, 2 * cap)

    yp, ys = _experts(idx_flat, h2, gate, w_gate, w_up, w_down, x2p, x2s, cap)
    return yp.reshape(bp, sp, dm), ys.reshape(bs, ss, dm)
```

```python
import functools
import math

import numpy as np
import jax
import jax.numpy as jnp
from jax import lax
from jax.experimental import pallas as pl
from jax.experimental.pallas import tpu as pltpu

BF = jnp.bfloat16
F32 = jnp.float32

EPS = 1e-6
NEG = -1e30
LANES = 128
SUBLANES = 8
HEAD_DIM = 128
ATTN_GROUPS = ((128, 1), (512, 4), (2048, 16))
HEADS_PER_GROUP = 4
GROUP_WIDTH = HEADS_PER_GROUP * HEAD_DIM
ATTN_WIDTH = len(ATTN_GROUPS) * GROUP_WIDTH
BAND = 64
ROPE_THETA = 500000.0
ROPE_DIM = HEAD_DIM // 4
SGU_CHUNK = 128
SGU_GROUPS = 4
SGU_WIDTH = 1536
SGU_GROUP_WIDTH = SGU_WIDTH // SGU_GROUPS
MEM_HEADS = 4
MEM_HEAD_DIM = 256
MEM_WIDTH = MEM_HEADS * MEM_HEAD_DIM
CAPACITY_FACTOR = 2
ATTN_SUB_BLOCK = 128
ATTN_Q_BLOCK = 512
INPROJ_ROW_SPLIT = 8
ROW_DMA_UNROLL = 8
DOWN_TILE = 256
VMEM_LIMIT = 52 * 1024 * 1024
EXPERT_VMEM_LIMIT = 60 * 1024 * 1024


def _params(n_axes, vmem=VMEM_LIMIT):
    return pltpu.CompilerParams(dimension_semantics=("arbitrary",) * n_axes, vmem_limit_bytes=vmem)


def _rms(x, gain):
    ms = jnp.mean(x * x, axis=-1, keepdims=True)
    return x * lax.rsqrt(ms + EPS) * gain


def _gelu_tanh(x):
    c = math.sqrt(2.0 / math.pi)
    return x * (0.5 * (1.0 + jnp.tanh(c * (x + 0.044715 * (x * x * x)))))


def _cast_kernel(x_ref, o_ref):
    o_ref[...] = x_ref[...].astype(BF)


def _cast_bf16(w, rows=512):
    r, c = w.shape
    rows = min(rows, r)
    return pl.pallas_call(
        _cast_kernel,
        grid=(r // rows,),
        in_specs=[pl.BlockSpec((rows, c), lambda i: (i, 0))],
        out_specs=pl.BlockSpec((rows, c), lambda i: (i, 0)),
        out_shape=jax.ShapeDtypeStruct((r, c), BF),
        compiler_params=_params(1),
        name="cast_bf16",
    )(w)


def _qk_norm_rope(acc, gain, cos_t, sin_t):
    lane = lax.broadcasted_iota(jnp.int32, (1, HEAD_DIM), 1)
    half = ROPE_DIM // 2
    outs = []
    for h in range(HEADS_PER_GROUP):
        y = _rms(acc[:, h * HEAD_DIM:(h + 1) * HEAD_DIM], gain)
        partner = jnp.where(lane < half, pltpu.roll(y, HEAD_DIM - half, 1), pltpu.roll(y, half, 1))
        outs.append(y * cos_t + partner * sin_t)
    return jnp.concatenate(outs, axis=1)


def _store_dilated(val, scr, o_ref, d, row0):
    n_rows = val.shape[0]
    if d == 1:
        o_ref[0, row0:row0 + n_rows, :] = val.astype(BF)
        return
    for h in range(HEADS_PER_GROUP):
        scr[h, row0:row0 + n_rows, :] = val[:, h * HEAD_DIM:(h + 1) * HEAD_DIM]
    for r in range(d):
        o_ref[r, row0 // d:(row0 + n_rows) // d, :] = jnp.concatenate(
            [scr[h, pl.ds(row0 + r, n_rows // d, stride=d), :] for h in range(HEADS_PER_GROUP)],
            axis=1).astype(BF)


def _norm1_kernel(xp_ref, xs_ref, g_ref, h_ref, *, n_prompt_tiles):
    i = pl.program_id(0)
    for x_ref, is_mine in ((xp_ref, i < n_prompt_tiles), (xs_ref, i >= n_prompt_tiles)):
        @pl.when(is_mine)
        def _(x_ref=x_ref):
            h_ref[...] = _rms(x_ref[...], g_ref[...]).astype(BF)


def _norm1(xp, xs, g1, tm=512):
    dm = xp.shape[1]
    m_tok = xp.shape[0] + xs.shape[0]
    npt = xp.shape[0] // tm
    return pl.pallas_call(
        functools.partial(_norm1_kernel, n_prompt_tiles=npt),
        grid=(m_tok // tm,),
        in_specs=[pl.BlockSpec((tm, dm), lambda i: (jnp.minimum(i, npt - 1), 0)),
                  pl.BlockSpec((tm, dm), lambda i: (jnp.maximum(i - npt, 0), 0)),
                  pl.BlockSpec((1, dm), lambda i: (0, 0))],
        out_specs=pl.BlockSpec((tm, dm), lambda i: (i, 0)),
        out_shape=jax.ShapeDtypeStruct((m_tok, dm), BF),
        compiler_params=_params(1),
        name="norm1",
    )(xp, xs, g1)


def _inproj_attn_kernel(h_ref, w_ref, qkg_ref, cos_ref, sin_ref, o0_ref, o1_ref, o2_ref, scr, *, tm):
    n = pl.program_id(1)
    group = n // 3
    which = n % 3
    w = w_ref[...].astype(BF)
    rq = tm // INPROJ_ROW_SPLIT

    def run(o_ref, d, rope):
        gain = qkg_ref[pl.ds(jnp.minimum(which, 1), 1), :]
        for b in range(INPROJ_ROW_SPLIT):
            rows = slice(b * rq, (b + 1) * rq)
            val = jnp.dot(h_ref[rows, :], w, preferred_element_type=F32)
            if rope:
                val = _qk_norm_rope(val, gain, cos_ref[rows, :], sin_ref[rows, :])
            _store_dilated(val, scr, o_ref, d, b * rq)

    for g, (o_ref, (_, d)) in enumerate(zip((o0_ref, o1_ref, o2_ref), ATTN_GROUPS)):
        @pl.when((group == g) & (which < 2))
        def _(o_ref=o_ref, d=d):
            run(o_ref, d, True)

        @pl.when((group == g) & (which == 2))
        def _(o_ref=o_ref, d=d):
            run(o_ref, d, False)


def _inproj_attn(h, w_in, qk_gain, cos_t, sin_t, seq_bounds, tm=1024):
    m_tok, dm = h.shape
    n_steps = 3 * len(ATTN_GROUPS)

    def pos_block(m):
        start = 0
        for bound in seq_bounds[1:-1]:
            start = jnp.where(m * tm >= bound, bound // tm, start)
        return m - start

    def out_spec(g, d):
        return pl.BlockSpec((None, d, tm // d, GROUP_WIDTH),
                            lambda m, n: (jnp.clip(n - 3 * g, 0, 2), 0, m, 0))

    return pl.pallas_call(
        functools.partial(_inproj_attn_kernel, tm=tm),
        grid=(m_tok // tm, n_steps),
        in_specs=[
            pl.BlockSpec((tm, dm), lambda m, n: (m, 0)),
            pl.BlockSpec((dm, GROUP_WIDTH), lambda m, n: (0, (n % 3) * 3 + n // 3)),
            pl.BlockSpec((2, HEAD_DIM), lambda m, n: (0, 0)),
            pl.BlockSpec((tm, HEAD_DIM), lambda m, n: (pos_block(m), 0)),
            pl.BlockSpec((tm, HEAD_DIM), lambda m, n: (pos_block(m), 0)),
        ],
        out_specs=[out_spec(g, d) for g, (_, d) in enumerate(ATTN_GROUPS)],
        out_shape=[jax.ShapeDtypeStruct((3, d, m_tok // d, GROUP_WIDTH), BF) for _, d in ATTN_GROUPS],
        scratch_shapes=[pltpu.VMEM((HEADS_PER_GROUP, tm, HEAD_DIM), F32)],
        compiler_params=_params(2),
        name="inproj_attn",
    )(h, w_in, qk_gain, cos_t, sin_t)


def _matmul_kernel(a_ref, w_ref, o_ref):
    o_ref[...] = jnp.dot(a_ref[...], w_ref[...].astype(BF),
                         preferred_element_type=F32).astype(o_ref.dtype)


def _inproj_rest(h, w_in, col0, tm=2048, tn=512):
    m_tok, dm = h.shape
    n_cols = w_in.shape[1] - col0
    tm = min(tm, m_tok)
    off = col0 // tn
    return pl.pallas_call(
        _matmul_kernel,
        grid=(m_tok // tm, n_cols // tn),
        in_specs=[pl.BlockSpec((tm, dm), lambda m, n: (m, 0)),
                  pl.BlockSpec((dm, tn), lambda m, n: (0, off + n))],
        out_specs=pl.BlockSpec((tm, tn), lambda m, n: (m, n)),
        out_shape=jax.ShapeDtypeStruct((m_tok, n_cols), BF),
        compiler_params=_params(2),
        name="inproj_rest",
    )(h, w_in)


def _attn_kernel(q_ref, kp_ref, kc_ref, kn_ref, vp_ref, vc_ref, vn_ref, o_ref, lse_ref, *, bounds, tq):
    sub = ATTN_SUB_BLOCK
    j0 = pl.program_id(1) * tq
    lo = jnp.int32(bounds[0])
    hi = jnp.int32(bounds[1])
    for b0, b1 in zip(bounds[1:-1], bounds[2:]):
        lo = jnp.where(j0 >= b0, b0, lo)
        hi = jnp.where(j0 >= b0, b1, hi)
    k_all = jnp.concatenate([kp_ref[...], kc_ref[...], kn_ref[...]], axis=0)
    v_all = jnp.concatenate([vp_ref[...], vc_ref[...], vn_ref[...]], axis=0)
    scale = 1.0 / math.sqrt(HEAD_DIM)
    lane = lax.broadcasted_iota(jnp.int32, (1, LANES), 1)
    n_keys = sub + 2 * BAND
    chains = [(u, h) for u in range(tq // sub) for h in range(HEADS_PER_GROUP)]

    def head(h):
        return slice(h * HEAD_DIM, (h + 1) * HEAD_DIM)

    def keys(u):
        return slice((u + 1) * sub - BAND, (u + 1) * sub - BAND + n_keys)

    valid = []
    for u in range(tq // sub):
        qpos = j0 + u * sub + lax.broadcasted_iota(jnp.int32, (sub, 1), 0)
        kpos = j0 + u * sub - BAND + lax.broadcasted_iota(jnp.int32, (1, n_keys), 1)
        valid.append((jnp.abs(kpos - qpos) <= BAND) & (kpos >= lo) & (kpos < hi))
    s = [lax.dot_general(q_ref[u * sub:(u + 1) * sub, head(h)], k_all[keys(u), head(h)],
                         (((1,), (1,)), ((), ())), preferred_element_type=F32) for u, h in chains]
    s = [jnp.where(valid[u], si * scale, NEG) for (u, h), si in zip(chains, s)]
    mx = [jnp.max(si, axis=-1, keepdims=True) for si in s]
    p = [jnp.exp(si - mi) for si, mi in zip(s, mx)]
    den = [jnp.sum(pi, axis=-1, keepdims=True) for pi in p]
    o = [jnp.dot(pi.astype(BF), v_all[keys(u), head(h)], preferred_element_type=F32)
         for (u, h), pi in zip(chains, p)]
    for u in range(tq // sub):
        rows = slice(u * sub, (u + 1) * sub)
        lse_all = jnp.zeros((sub, LANES), F32)
        for h in range(HEADS_PER_GROUP):
            c = chains.index((u, h))
            o_ref[rows, head(h)] = (o[c] / den[c]).astype(BF)
            lse_all = jnp.where(lane == h, mx[c] + jnp.log(den[c]), lse_all)
        lse_ref[rows, :] = lse_all


def _band_attention(qkv, d, seq_bounds):
    _, _, n_j, _ = qkv.shape
    sub = ATTN_SUB_BLOCK
    bounds = tuple(b // d for b in seq_bounds)
    tq = max(t for t in (sub, 2 * sub, ATTN_Q_BLOCK) if all(b % t == 0 for b in bounds))
    n_blocks = n_j // tq
    n_sub = n_j // sub
    per = tq // sub

    def spec(which, shift):
        if shift == 0:
            return pl.BlockSpec((None, None, tq, GROUP_WIDTH), lambda r, i: (which, r, i, 0))
        halo = -1 if shift < 0 else per
        return pl.BlockSpec((None, None, sub, GROUP_WIDTH),
                            lambda r, i: (which, r, jnp.clip(i * per + halo, 0, n_sub - 1), 0))

    return pl.pallas_call(
        functools.partial(_attn_kernel, bounds=bounds, tq=tq),
        grid=(d, n_blocks),
        in_specs=[spec(0, 0), spec(1, -1), spec(1, 0), spec(1, 1), spec(2, -1), spec(2, 0), spec(2, 1)],
        out_specs=[pl.BlockSpec((None, tq, GROUP_WIDTH), lambda r, i: (r, i, 0)),
                   pl.BlockSpec((None, tq, LANES), lambda r, i: (r, i, 0))],
        out_shape=[jax.ShapeDtypeStruct((d, n_j, GROUP_WIDTH), BF),
                   jax.ShapeDtypeStruct((d, n_j, LANES), F32)],
        compiler_params=_params(2),
        name=f"band_attention_d{d}",
    )(qkv, qkv, qkv, qkv, qkv, qkv, qkv)


def _sgu_kernel(u_ref, v_ref, ng_ref, ws_ref, bs_ref, o_ref, *, tb):
    vn = _rms(_gelu_tanh(v_ref[...].astype(F32)), ng_ref[...]).astype(BF)
    for g in range(SGU_GROUPS):
        cols = slice(g * SGU_GROUP_WIDTH, (g + 1) * SGU_GROUP_WIDTH)
        w = ws_ref[g].astype(BF)
        bias = bs_ref[:, g:g + 1]
        for c in range(tb // SGU_CHUNK):
            rows = slice(c * SGU_CHUNK, (c + 1) * SGU_CHUNK)
            mixed = jnp.dot(w, vn[rows, cols], preferred_element_type=F32) + bias
            u = _gelu_tanh(u_ref[rows, cols].astype(F32))
            o_ref[rows, cols] = (u * mixed).astype(BF)


def _sgu(z_rest, norm_g, w_s, b_s_t, tb=512):
    m_tok = z_rest.shape[0]
    return pl.pallas_call(
        functools.partial(_sgu_kernel, tb=tb),
        grid=(m_tok // tb,),
        in_specs=[pl.BlockSpec((tb, SGU_WIDTH), lambda i: (i, 0)),
                  pl.BlockSpec((tb, SGU_WIDTH), lambda i: (i, 1)),
                  pl.BlockSpec((1, SGU_WIDTH), lambda i: (0, 0)),
                  pl.BlockSpec((SGU_GROUPS, SGU_CHUNK, SGU_CHUNK), lambda i: (0, 0, 0)),
                  pl.BlockSpec((SGU_CHUNK, SGU_GROUPS), lambda i: (0, 0))],
        out_specs=pl.BlockSpec((tb, SGU_WIDTH), lambda i: (i, 0)),
        out_shape=jax.ShapeDtypeStruct((m_tok, SGU_WIDTH), BF),
        compiler_params=_params(1),
        name="sgu",
    )(z_rest, z_rest, norm_g, w_s, b_s_t)


def _mem_kv_kernel(mem_ref, ng_ref, w_ref, kg_ref, o_ref, *, n_k_tiles, tn):
    n = pl.program_id(1)
    hm = _rms(mem_ref[...], ng_ref[...]).astype(BF)
    kv = jnp.dot(hm, w_ref[...].astype(BF), preferred_element_type=F32)

    @pl.when(n < n_k_tiles)
    def _():
        for h in range(tn // MEM_HEAD_DIM):
            cols = slice(h * MEM_HEAD_DIM, (h + 1) * MEM_HEAD_DIM)
            o_ref[:, cols] = _rms(kv[:, cols], kg_ref[...]).astype(BF)

    @pl.when(n >= n_k_tiles)
    def _():
        o_ref[...] = kv.astype(BF)


def _mem_kv(mem, norm_g, w_kv, k_gain, tn=512):
    n_b, n_mem, dm = mem.shape
    width = w_kv.shape[1]
    return pl.pallas_call(
        functools.partial(_mem_kv_kernel, n_k_tiles=MEM_WIDTH // tn, tn=tn),
        grid=(n_b, width // tn),
        in_specs=[pl.BlockSpec((None, n_mem, dm), lambda b, n: (b, 0, 0)),
                  pl.BlockSpec((1, dm), lambda b, n: (0, 0)),
                  pl.BlockSpec((dm, tn), lambda b, n: (0, n)),
                  pl.BlockSpec((1, MEM_HEAD_DIM), lambda b, n: (0, 0))],
        out_specs=pl.BlockSpec((None, n_mem, tn), lambda b, n: (b, 0, n)),
        out_shape=jax.ShapeDtypeStruct((n_b, n_mem, width), BF),
        compiler_params=_params(2),
        name="mem_kv",
    )(mem, norm_g, w_kv, k_gain)


def _mem_attn_kernel(q_ref, kv_ref, qg_ref, o_ref):
    scale = 1.0 / math.sqrt(MEM_HEAD_DIM)
    heads = [slice(h * MEM_HEAD_DIM, (h + 1) * MEM_HEAD_DIM) for h in range(MEM_HEADS)]
    nt = (((1,), (1,)), ((), ()))
    q = [_rms(q_ref[:, c].astype(F32), qg_ref[...]).astype(BF) for c in heads]
    s = [lax.dot_general(qh, kv_ref[:, c], nt, preferred_element_type=F32) * scale for qh, c in zip(q, heads)]
    mx = [jnp.max(sh, axis=-1, keepdims=True) for sh in s]
    p = [jnp.exp(sh - mh) for sh, mh in zip(s, mx)]
    den = [jnp.sum(ph, axis=-1, keepdims=True) for ph in p]
    o = [jnp.dot(ph.astype(BF), kv_ref[:, MEM_WIDTH + c.start:MEM_WIDTH + c.stop], preferred_element_type=F32)
         for ph, c in zip(p, heads)]
    for c, oh, dh in zip(heads, o, den):
        o_ref[:, c] = (oh / dh).astype(BF)


def _mem_attn(z_rest, kv, q_gain, batch_bounds, tm=512):
    m_tok = z_rest.shape[0]
    n_mem = kv.shape[1]
    q_block = (2 * SGU_WIDTH) // MEM_WIDTH

    def batch_of(i):
        b = 0
        for bound in batch_bounds[1:-1]:
            b = b + (i * tm >= bound).astype(jnp.int32)
        return b

    return pl.pallas_call(
        _mem_attn_kernel,
        grid=(m_tok // tm,),
        in_specs=[pl.BlockSpec((tm, MEM_WIDTH), lambda i: (i, q_block)),
                  pl.BlockSpec((None, n_mem, 2 * MEM_WIDTH), lambda i: (batch_of(i), 0, 0)),
                  pl.BlockSpec((1, MEM_HEAD_DIM), lambda i: (0, 0))],
        out_specs=pl.BlockSpec((tm, MEM_WIDTH), lambda i: (i, 0)),
        out_shape=jax.ShapeDtypeStruct((m_tok, MEM_WIDTH), BF),
        compiler_params=_params(1),
        name="mem_attn",
    )(z_rest, kv, q_gain)


def _merge_kernel(oa0_ref, oa1_ref, oa2_ref, l0_ref, l1_ref, l2_ref, ob_ref, om_ref,
                  g0_ref, g1_ref, g2_ref, wa_ref, wb_ref, wm_ref, o_ref, oa_scr, lse_scr, *, tm):
    n_h = HEADS_PER_GROUP
    n_g = len(ATTN_GROUPS)
    for g, (oa_ref, l_ref, (_, d)) in enumerate(zip((oa0_ref, oa1_ref, oa2_ref),
                                                    (l0_ref, l1_ref, l2_ref), ATTN_GROUPS)):
        for r in range(d):
            rows = pl.ds(r, tm // d, stride=d) if d > 1 else slice(None)
            o = oa_ref[r].astype(F32)
            for h in range(n_h):
                oa_scr[g * n_h + h, rows, :] = o[:, h * HEAD_DIM:(h + 1) * HEAD_DIM]
            lse_scr[g, rows, :] = l_ref[r]
    lses = [lse_scr[g] for g in range(n_g)]
    mx = jnp.maximum(jnp.maximum(lses[0], lses[1]), lses[2])
    es = [jnp.exp(l - mx) for l in lses]
    den = es[0] + es[1] + es[2]
    slabs = []
    for g in range(n_g):
        alpha = es[g] / den
        slabs += [(oa_scr[g * n_h + h] * alpha[:, h:h + 1]).astype(BF) for h in range(n_h)]
    pa = jnp.dot(jnp.concatenate(slabs, axis=1), wa_ref[...], preferred_element_type=F32)
    merged = jax.nn.sigmoid(g0_ref[...].astype(F32)) * pa
    pb = jnp.dot(ob_ref[...], wb_ref[...], preferred_element_type=F32)
    merged = merged + jax.nn.sigmoid(g1_ref[...].astype(F32)) * pb
    pm = jnp.dot(om_ref[...], wm_ref[...], preferred_element_type=F32)
    merged = merged + jax.nn.sigmoid(g2_ref[...].astype(F32)) * pm
    o_ref[...] = merged.astype(BF)


def _resident(shape):
    return pl.BlockSpec(shape, lambda i: (0,) * len(shape), pipeline_mode=pl.Buffered(1))


def _merge(oas, lses, ob, om, z_rest, wa, wb, wm, dm, tm=512):
    m_tok = ob.shape[0]
    gate0 = (2 * SGU_WIDTH + MEM_WIDTH) // dm
    oa_specs = [pl.BlockSpec((d, tm // d, GROUP_WIDTH), lambda i: (0, i, 0)) for _, d in ATTN_GROUPS]
    l_specs = [pl.BlockSpec((d, tm // d, LANES), lambda i: (0, i, 0)) for _, d in ATTN_GROUPS]
    g_specs = [pl.BlockSpec((tm, dm), lambda i, b=b: (i, gate0 + b)) for b in range(3)]
    return pl.pallas_call(
        functools.partial(_merge_kernel, tm=tm),
        grid=(m_tok // tm,),
        in_specs=oa_specs + l_specs
        + [pl.BlockSpec((tm, SGU_WIDTH), lambda i: (i, 0)), pl.BlockSpec((tm, MEM_WIDTH), lambda i: (i, 0))]
        + g_specs + [_resident(wa.shape), _resident(wb.shape), _resident(wm.shape)],
        out_specs=pl.BlockSpec((tm, dm), lambda i: (i, 0)),
        out_shape=jax.ShapeDtypeStruct((m_tok, dm), BF),
        scratch_shapes=[pltpu.VMEM((len(ATTN_GROUPS) * HEADS_PER_GROUP, tm, HEAD_DIM), F32),
                        pltpu.VMEM((len(ATTN_GROUPS), tm, LANES), F32)],
        compiler_params=_params(1),
        name="merge",
    )(*oas, *lses, ob, om, z_rest, z_rest, z_rest, wa, wb, wm)


def _out_kernel(mg_ref, xp_in, xs_in, wo_ref, g2_ref, wr_ref, xp_ref, xs_ref, h2_ref, lg_ref, *,
                n_prompt_tiles):
    i = pl.program_id(0)
    tm = mg_ref.shape[0]
    halves = [slice(u * (tm // 2), (u + 1) * (tm // 2)) for u in range(2)]
    nt = (((1,), (1,)), ((), ()))

    def run(x_in, x_out):
        wr = wr_ref[...]
        w_hi = wr.astype(BF)
        w_lo = (wr - w_hi.astype(F32)).astype(BF)
        proj = [jnp.dot(mg_ref[r, :], wo_ref[...], preferred_element_type=F32) for r in halves]
        x2 = [x_in[r, :] + p for r, p in zip(halves, proj)]
        for r, v in zip(halves, x2):
            x_out[r, :] = v
        h2 = [_rms(v, g2_ref[...]) for v in x2]
        for r, v in zip(halves, h2):
            for k in range(v.shape[1] // (SUBLANES * LANES)):
                for j in range(SUBLANES):
                    c = (k * SUBLANES + j) * LANES
                    h2_ref[k, pl.ds(r.start * SUBLANES + j, r.stop - r.start, stride=SUBLANES), :] = (
                        v[:, c:c + LANES])
        for r, v in zip(halves, h2):
            h_hi = v.astype(BF)
            h_lo = (v - h_hi.astype(F32)).astype(BF)
            lg = lax.dot_general(w_hi, h_hi, nt, preferred_element_type=F32)
            lg = lg + lax.dot_general(w_hi, h_lo, nt, preferred_element_type=F32)
            lg = lg + lax.dot_general(w_lo, h_hi, nt, preferred_element_type=F32)
            lg_ref[:, r] = lg

    @pl.when(i < n_prompt_tiles)
    def _():
        run(xp_in, xp_ref)

    @pl.when(i >= n_prompt_tiles)
    def _():
        run(xs_in, xs_ref)


def _out_proj(merged, xp, xs, wo, g2, w_router_t, tm=256):
    n_prompt, dm = xp.shape
    m_tok = n_prompt + xs.shape[0]
    n_e = w_router_t.shape[0]
    npt = n_prompt // tm
    n_tiles = dm // (SUBLANES * LANES)
    return pl.pallas_call(
        functools.partial(_out_kernel, n_prompt_tiles=npt),
        grid=(m_tok // tm,),
        in_specs=[pl.BlockSpec((tm, dm), lambda i: (i, 0)),
                  pl.BlockSpec((tm, dm), lambda i: (jnp.minimum(i, npt - 1), 0)),
                  pl.BlockSpec((tm, dm), lambda i: (jnp.maximum(i - npt, 0), 0)),
                  _resident(wo.shape),
                  pl.BlockSpec((1, dm), lambda i: (0, 0)),
                  pl.BlockSpec((n_e, dm), lambda i: (0, 0))],
        out_specs=[pl.BlockSpec((tm, dm), lambda i: (jnp.minimum(i, npt - 1), 0)),
                   pl.BlockSpec((tm, dm), lambda i: (jnp.maximum(i - npt, 0), 0)),
                   pl.BlockSpec((n_tiles, tm * SUBLANES, LANES), lambda i: (0, i, 0)),
                   pl.BlockSpec((n_e, tm), lambda i: (0, i))],
        out_shape=[jax.ShapeDtypeStruct((n_prompt, dm), F32),
                   jax.ShapeDtypeStruct((m_tok - n_prompt, dm), F32),
                   jax.ShapeDtypeStruct((n_tiles, m_tok * SUBLANES, LANES), F32),
                   jax.ShapeDtypeStruct((n_e, m_tok), F32)],
        compiler_params=_params(1),
        name="out_proj",
    )(merged, xp, xs, wo, g2, w_router_t)


def _router_kernel(lg_ref, u_ref, bl_ref, ui_ref, idx_ref, gate_ref, *, cap):
    n_e, n_r, _ = lg_ref.shape
    lg = lg_ref[...]
    ex = jnp.exp(lg - jnp.max(lg, axis=0, keepdims=True))
    aff = ex / jnp.sum(ex, axis=0, keepdims=True)

    def count(mask):
        return jnp.sum(jnp.sum(mask.astype(F32), axis=2, keepdims=True), axis=1, keepdims=True)

    thr_bits = jnp.zeros((n_e, 1, 1), jnp.int32)
    for bit in range(30, -1, -1):
        cand = thr_bits | jnp.int32(1 << bit)
        enough = count(aff >= lax.bitcast_convert_type(cand, F32)) >= cap
        thr_bits = jnp.where(enough, cand, thr_bits)
    thr = lax.bitcast_convert_type(thr_bits, F32)
    gt = aff > thr
    eq = aff == thr
    need = cap - count(gt)

    def prefix(mask_f32):
        m2 = mask_f32.reshape(n_e * n_r, LANES)
        incl = jnp.dot(m2.astype(BF), u_ref[...], preferred_element_type=F32)
        tot = jnp.broadcast_to(incl[:, LANES - 1:LANES], incl.shape)
        rowoff = jnp.dot(bl_ref[...], tot.astype(BF), preferred_element_type=F32)
        return rowoff + incl - m2, incl

    eq_rank, _ = prefix(eq.astype(F32))
    take_eq = eq & (eq_rank.reshape(n_e, n_r, LANES) < need)
    sel = jnp.where(gt, 1.0, jnp.where(take_eq, 1.0, 0.0))
    _, incl = prefix(sel)
    sel2 = sel.reshape(n_e * n_r, LANES).astype(BF)

    def to_row(col):
        return jnp.transpose(jnp.broadcast_to(col, (cap, LANES)))[0:1, :]

    ones8 = jnp.ones((8, LANES), BF)
    slot = lax.broadcasted_iota(jnp.int32, (cap, 1), 0).astype(F32)
    row_id = lax.broadcasted_iota(jnp.int32, (cap, n_r), 1).astype(F32)
    lane_id = lax.broadcasted_iota(jnp.int32, (cap, LANES), 1).astype(F32)
    nt = (((1,), (1,)), ((), ()))
    for e in range(n_e):
        rows = slice(e * n_r, (e + 1) * n_r)
        tot_row = lax.dot_general(ones8, sel2[rows], nt, preferred_element_type=F32)[0:1]
        cum_row = jnp.dot(tot_row.astype(BF), ui_ref[...], preferred_element_type=F32)
        r_s = jnp.sum((cum_row <= slot).astype(F32), axis=1, keepdims=True)
        onehot = row_id == r_s
        oh = jnp.where(onehot, 1.0, 0.0).astype(BF)
        row_start = jnp.sum(jnp.where(onehot, cum_row - tot_row, 0.0), axis=1, keepdims=True)
        g_incl = jnp.dot(oh, incl[rows].astype(BF), preferred_element_type=F32)
        l_s = jnp.sum((g_incl <= slot - row_start).astype(F32), axis=1, keepdims=True)
        idx_ref[e:e + 1, :] = to_row(r_s * LANES + l_s).astype(jnp.int32)
        a = aff[e]
        a1 = a.astype(BF)
        a2 = (a - a1.astype(F32)).astype(BF)
        a3 = (a - a1.astype(F32) - a2.astype(F32)).astype(BF)
        g_aff = (jnp.dot(oh, a1, preferred_element_type=F32)
                 + jnp.dot(oh, a2, preferred_element_type=F32)) + jnp.dot(oh, a3, preferred_element_type=F32)
        gate_ref[e:e + 1, :] = to_row(jnp.sum(jnp.where(lane_id == l_s, g_aff, 0.0), axis=1, keepdims=True))


def _route(logits, cap):
    n_g, n_e, n_r, _ = logits.shape
    a = np.arange(LANES)
    upper = jnp.asarray(a[:, None] <= a[None, :], BF)
    i = np.arange(n_e * n_r)
    block_lower = jnp.asarray((i[:, None] // n_r == i[None, :] // n_r) & (i[None, :] < i[:, None]), BF)
    r = np.arange(n_r)
    upper_r = jnp.asarray(r[:, None] <= r[None, :], BF)
    return pl.pallas_call(
        functools.partial(_router_kernel, cap=cap),
        grid=(n_g,),
        in_specs=[pl.BlockSpec((None, n_e, n_r, LANES), lambda g: (g, 0, 0, 0)),
                  pl.BlockSpec((LANES, LANES), lambda g: (0, 0)),
                  pl.BlockSpec((n_e * n_r, n_e * n_r), lambda g: (0, 0)),
                  pl.BlockSpec((n_r, n_r), lambda g: (0, 0))],
        out_specs=[pl.BlockSpec((None, n_e, cap), lambda g: (g, 0, 0)),
                   pl.BlockSpec((None, n_e, cap), lambda g: (g, 0, 0))],
        out_shape=[jax.ShapeDtypeStruct((n_g, n_e, cap), jnp.int32),
                   jax.ShapeDtypeStruct((n_g, n_e, cap), F32)],
        compiler_params=_params(1),
        name="router",
    )(logits, upper, block_lower, upper_r)


def _expert_kernel(idx_ref, h2_hbm, gate_ref, wg_ref, wu_ref, wd_ref, xp_in, xs_in,
                   yp_hbm, ys_hbm, xb, xg, stage, acc, sem_x, sem_in, sem_out, *,
                   cap, n_group_tok, n_f, rows_per_step, res_rows_per_step):
    del xp_in, xs_in
    e = pl.program_id(0)
    g = pl.program_id(1)
    f = pl.program_id(2)
    unit = 2 * e + g
    n_units = 2 * pl.num_programs(0)
    n_tiles = xg.shape[0]
    n_in_step = n_f * res_rows_per_step
    y_hbm = (yp_hbm, ys_hbm)

    def x_row_copy(u, s):
        t = idx_ref[u * cap + jnp.minimum(s, cap - 1)]
        src = h2_hbm.at[:, pl.ds(pl.multiple_of(t * SUBLANES, SUBLANES), SUBLANES)]
        dst = xg.at[:, pl.ds(pl.multiple_of(s * SUBLANES, SUBLANES), SUBLANES)]
        pltpu.make_async_copy(src, dst, sem_x).start()

    def x_rows_wait(n_rows):
        pltpu.make_async_copy(h2_hbm.at[:, pl.ds(0, n_rows * SUBLANES)],
                              xg.at[:, pl.ds(0, n_rows * SUBLANES)], sem_x).wait()

    def res_row_copy(grp, u, slot, to_vmem):
        t = idx_ref[u * cap + slot] - grp * n_group_tok
        src, dst = y_hbm[grp].at[pl.ds(t, 1)], stage.at[grp, pl.ds(slot, 1)]
        if not to_vmem:
            src, dst = dst, src
        pltpu.make_async_copy(src, dst, (sem_in if to_vmem else sem_out).at[grp]).start(
            priority=0 if to_vmem else 1)

    def res_rows_loop(grp, u, lo, hi, to_vmem):
        def body(i, carry):
            s0 = pl.multiple_of(lo + i * ROW_DMA_UNROLL, ROW_DMA_UNROLL)
            for j in range(ROW_DMA_UNROLL):
                res_row_copy(grp, u, s0 + j, to_vmem)
            return carry
        lax.fori_loop(0, (hi - lo) // ROW_DMA_UNROLL, body, 0)

    def res_rows_wait(grp, to_vmem):
        src, dst = y_hbm[grp].at[pl.ds(0, cap)], stage.at[grp]
        if not to_vmem:
            src, dst = dst, src
        pltpu.make_async_copy(src, dst, (sem_in if to_vmem else sem_out).at[grp]).wait()

    @pl.when((f == 0) & (unit == 0))
    def _():
        def body(i, carry):
            for j in range(ROW_DMA_UNROLL):
                x_row_copy(0, i * ROW_DMA_UNROLL + j)
            return carry
        lax.fori_loop(0, cap // ROW_DMA_UNROLL, body, 0)
        x_rows_wait(cap)

    @pl.when((f == 0) & (unit > 0))
    def _():
        for _ in range(n_f):
            x_rows_wait(rows_per_step)

    for grp in range(2):
        @pl.when((f == 0) & (g == grp))
        def _(grp=grp):
            for k in range(n_tiles):
                for j in range(SUBLANES):
                    c = (k * SUBLANES + j) * LANES
                    xb[:, c:c + LANES] = xg[k, pl.ds(j, cap, stride=SUBLANES), :].astype(BF)
            acc[...] = jnp.zeros_like(acc)
            res_rows_loop(grp, unit, n_in_step, cap, True)

    def main_block(grp, scatter_prev):
        nxt = jnp.minimum(unit + 1, n_units - 1)
        for j in range(rows_per_step):
            x_row_copy(nxt, f * rows_per_step + j)
        for j in range(res_rows_per_step):
            s = f * res_rows_per_step + j
            res_row_copy(grp, unit, s, True)
            if scatter_prev:
                res_row_copy(1 - grp, unit - 1, s, False)
        x = xb[...]
        a = jnp.dot(x, wg_ref[...].astype(BF), preferred_element_type=F32)
        b = jnp.dot(x, wu_ref[...].astype(BF), preferred_element_type=F32)
        hmid = (a * jax.nn.sigmoid(a) * b).astype(BF)
        dm = acc.shape[1]
        for n in range(dm // DOWN_TILE):
            cols = slice(n * DOWN_TILE, (n + 1) * DOWN_TILE)
            acc[:, cols] += jnp.dot(hmid, wd_ref[:, cols].astype(BF), preferred_element_type=F32)

    @pl.when(unit == 0)
    def _():
        main_block(0, False)

    @pl.when((g == 0) & (e > 0))
    def _():
        main_block(0, True)

    @pl.when(g == 1)
    def _():
        main_block(1, True)

    for grp in range(2):
        @pl.when((f == n_f - 1) & (g == grp))
        def _(grp=grp):
            res_rows_wait(grp, True)
            half = cap // 2
            for r in range(2):
                rows = slice(r * half, (r + 1) * half)
                gate_col = jnp.transpose(jnp.broadcast_to(gate_ref[:, rows], (LANES, half)))[:, 0:1]
                stage[grp, rows, :] = stage[grp, rows, :] + acc[rows, :] * gate_col
            res_rows_loop(grp, unit, n_in_step, cap, False)

        @pl.when((f == n_f - 1) & (g == grp) & (unit > 0))
        def _(grp=grp):
            res_rows_wait(1 - grp, False)

    @pl.when((f == n_f - 1) & (unit == n_units - 1))
    def _():
        res_rows_loop(1, unit, 0, n_in_step, False)
        res_rows_wait(1, False)
        for _ in range(n_f):
            x_rows_wait(rows_per_step)


def _experts(idx_flat, h2, gate, w_gate, w_up, w_down, x2p, x2s, cap, tf=256):
    n_e, dm, ff = w_gate.shape
    n_group_tok = x2p.shape[0]
    n_f = ff // tf
    rows_per_step = -(-cap // n_f)
    n_tiles = dm // (SUBLANES * LANES)
    unit_rows = ROW_DMA_UNROLL // math.gcd(n_f, ROW_DMA_UNROLL)
    res_rows_per_step = (cap // n_f) // unit_rows * unit_rows
    any_spec = pl.BlockSpec(memory_space=pl.ANY)
    grid_spec = pltpu.PrefetchScalarGridSpec(
        num_scalar_prefetch=1,
        grid=(n_e, 2, n_f),
        in_specs=[any_spec,
                  pl.BlockSpec((None, 1, cap), lambda e, g, f, idx: (e, 0, g)),
                  pl.BlockSpec((None, dm, tf), lambda e, g, f, idx: (e, 0, f)),
                  pl.BlockSpec((None, dm, tf), lambda e, g, f, idx: (e, 0, f)),
                  pl.BlockSpec((None, tf, dm), lambda e, g, f, idx: (e, f, 0)),
                  any_spec, any_spec],
        out_specs=[any_spec, any_spec],
        scratch_shapes=[pltpu.VMEM((cap, dm), BF),
                        pltpu.VMEM((n_tiles, n_f * rows_per_step * SUBLANES, LANES), F32),
                        pltpu.VMEM((2, cap, dm), F32),
                        pltpu.VMEM((cap, dm), F32),
                        pltpu.SemaphoreType.DMA(()),
                        pltpu.SemaphoreType.DMA((2,)),
                        pltpu.SemaphoreType.DMA((2,))],
    )
    return pl.pallas_call(
        functools.partial(_expert_kernel, cap=cap, n_group_tok=n_group_tok, n_f=n_f,
                          rows_per_step=rows_per_step, res_rows_per_step=res_rows_per_step),
        grid_spec=grid_spec,
        out_shape=[jax.ShapeDtypeStruct(x2p.shape, F32), jax.ShapeDtypeStruct(x2s.shape, F32)],
        input_output_aliases={6: 0, 7: 1},
        compiler_params=_params(3, EXPERT_VMEM_LIMIT),
        name="experts",
    )(idx_flat, h2, gate, w_gate, w_up, w_down, x2p, x2s)


def _rope_tables(max_len):
    half = ROPE_DIM // 2
    inv = ROPE_THETA ** (-np.arange(0, ROPE_DIM, 2, dtype=np.float64) / ROPE_DIM)
    ang = np.arange(max_len, dtype=np.float64)[:, None] * inv[None, :]
    cos, sin = np.cos(ang), np.sin(ang)
    cos_t = np.concatenate([cos, cos, np.ones((max_len, HEAD_DIM - 2 * half))], axis=1)
    sin_t = np.concatenate([-sin, sin, np.zeros((max_len, HEAD_DIM - 2 * half))], axis=1)
    return jnp.asarray(cos_t, F32), jnp.asarray(sin_t, F32)


def kernel(x_prompt, x_sample, mem_prompt, mem_sample, norm1_g, w_in, q_norm_a, k_norm_a, sgu_norm_g,
           sgu_w, sgu_b, mem_norm_g, w_mem_kv, q_norm_m, k_norm_m, w_proj_a, w_proj_b, w_proj_m, w_out,
           norm2_g, w_router, w_gate, w_up, w_down):
    assert norm1_g.shape[0] == 1, "one layer"
    bp, sp, dm = x_prompt.shape
    bs, ss, _ = x_sample.shape
    n_prompt, n_sample = bp * sp, bs * ss
    assert n_prompt == n_sample, "both request groups route the same number of tokens"
    n_e = w_router.shape[-1]
    cap = CAPACITY_FACTOR * n_prompt // n_e
    seq_lens = (sp,) * bp + (ss,) * bs
    seq_bounds = tuple(int(v) for v in np.concatenate([[0], np.cumsum(seq_lens)]))

    xp = x_prompt.reshape(n_prompt, dm)
    xs = x_sample.reshape(n_sample, dm)
    mem = jnp.concatenate([mem_prompt, mem_sample], axis=0)
    cos_t, sin_t = _rope_tables(max(seq_lens))
    qk_gain = jnp.concatenate([q_norm_a, k_norm_a], axis=0)

    def layer0(w):
        return w.reshape(w.shape[1:])

    w_in, sgu_w, sgu_b, w_mem_kv, w_proj_a, w_proj_b, w_proj_m, w_out, w_router, w_gate, w_up, w_down = map(
        layer0, (w_in, sgu_w, sgu_b, w_mem_kv, w_proj_a, w_proj_b, w_proj_m, w_out, w_router, w_gate, w_up,
                 w_down))

    h = _norm1(xp, xs, norm1_g)
    qkvs = _inproj_attn(h, w_in, qk_gain, cos_t, sin_t, seq_bounds)
    z_rest = _inproj_rest(h, w_in, 3 * ATTN_WIDTH)

    oas, lses = [], []
    for qkv, (_, d) in zip(qkvs, ATTN_GROUPS):
        o, lse = _band_attention(qkv, d, seq_bounds)
        oas.append(o)
        lses.append(lse)
    ob = _sgu(z_rest, sgu_norm_g, sgu_w, sgu_b.T)
    kv = _mem_kv(mem, mem_norm_g, w_mem_kv, k_norm_m)
    om = _mem_attn(z_rest, kv, q_norm_m, seq_bounds)

    merged = _merge(oas, lses, ob, om, z_rest, _cast_bf16(w_proj_a), _cast_bf16(w_proj_b),
                    _cast_bf16(w_proj_m), dm)
    x2p, x2s, h2, logits_t = _out_proj(merged, xp, xs, _cast_bf16(w_out), norm2_g, w_router.T)

    logits = logits_t.reshape(n_e, 2, n_prompt // LANES, LANES).transpose(1, 0, 2, 3)
    idx, gate = _route(logits, cap)
    tok_off = jnp.arange(2, dtype=jnp.int32)[:, None, None] * n_prompt
    idx_flat = (idx + tok_off).transpose(1, 0, 2).reshape(-1)
    gate = gate.transpose(1, 0, 2).reshape(n_e, 1, 2 * cap)

    yp, ys = _experts(idx_flat, h2, gate, w_gate, w_up, w_down, x2p, x2s, cap)
    return yp.reshape(bp, sp, dm), ys.reshape(bs, ss, dm)
```
